```python
import jax, jax.numpy as jnp
from jax import lax
import numpy as np

D_MODEL = 1024
BATCH = 2
SEQ = 8192
DEPTH = 1

D_MIX = D_MODEL
D_A = D_MIX // 2
D_B = D_MIX - D_A
N_HEADS_A = 8
HEAD_DIM_A = D_A // N_HEADS_A
N_GROUPS_B = 8
GROUP_DIM_B = D_B // N_GROUPS_B
CHUNK = 128
CONV_WIDTH = 31
D_IN = 2 * D_A + 2 * D_B
D_FF = -(-8 * D_MODEL // (3 * 256)) * 256
N_MOD = 6
EPS = 1e-6

kernel_name = "hybrid_sgu_conformer_conv_adaln_block"


def rmsnorm(x, g):
    xf = x.astype(jnp.float32)
    y = xf * lax.rsqrt(jnp.mean(xf * xf, axis=-1, keepdims=True) + EPS)
    return (y * g.astype(jnp.float32)).astype(x.dtype)


def layernorm(x, g, b):
    xf = x.astype(jnp.float32)
    mu = jnp.mean(xf, axis=-1, keepdims=True)
    xc = xf - mu
    var = jnp.mean(xc * xc, axis=-1, keepdims=True)
    y = xc * lax.rsqrt(var + EPS) * g.astype(jnp.float32) + b.astype(jnp.float32)
    return y.astype(x.dtype)


def modulate(h, shift, scale):
    return h * (1 + scale[:, None, :]) + shift[:, None, :]


def spatial_gating_mixer(u, v, ln_g, ln_b, w_s, b_s):
    B, S, _ = v.shape
    v = layernorm(v, ln_g, ln_b)
    v = v.reshape(B, S // CHUNK, CHUNK, N_HEADS_A, HEAD_DIM_A)
    causal = jnp.tril(jnp.ones((CHUNK, CHUNK), dtype=bool))
    w = jnp.where(causal[None], w_s, jnp.zeros_like(w_s)).astype(v.dtype)
    mixed = jnp.einsum('hts,bcshd->bcthd', w, v) + b_s.T.astype(v.dtype)[None, None, :, :, None]
    return u * mixed.reshape(B, S, D_A)


def conformer_conv_mixer(val, gate, conv_w, conv_b, gn_g, gn_b):
    B, S, _ = val.shape
    y = val * jax.nn.sigmoid(gate)
    y = lax.conv_general_dilated(
        y, conv_w[:, None, :].astype(y.dtype), window_strides=(1,),
        padding=[(CONV_WIDTH - 1, 0)],
        dimension_numbers=('NWC', 'WIO', 'NWC'),
        feature_group_count=D_B) + conv_b.astype(y.dtype)
    y = y.reshape(B, S, N_GROUPS_B, GROUP_DIM_B)
    y = layernorm(y, gn_g.reshape(N_GROUPS_B, GROUP_DIM_B), gn_b.reshape(N_GROUPS_B, GROUP_DIM_B))
    return jax.nn.silu(y.reshape(B, S, D_B))


def setup_inputs(seed: int = 0) -> dict:
    key = jax.random.key(seed)
    ks = jax.random.split(key, 24)
    f32 = jnp.float32
    nrm = lambda k, shape, s: jax.random.normal(k, shape, f32) * s
    L = DEPTH
    return {
        "x": jax.random.normal(ks[0], (BATCH, SEQ, D_MODEL), f32),
        "c": jax.random.normal(ks[1], (BATCH, D_MODEL), f32),
        "ada_w": nrm(ks[2], (L, D_MODEL, N_MOD * D_MODEL), D_MODEL ** -0.5),
        "ada_b": nrm(ks[3], (L, N_MOD * D_MODEL), 0.02),
        "norm1_g": 1.0 + nrm(ks[4], (L, D_MODEL), 0.05),
        "w_in": nrm(ks[5], (L, D_MODEL, D_IN), D_MODEL ** -0.5),
        "b_in": nrm(ks[6], (L, D_IN), 0.02),
        "a_ln_g": 1.0 + nrm(ks[7], (L, D_A), 0.05),
        "a_ln_b": nrm(ks[8], (L, D_A), 0.02),
        "a_spatial_w": nrm(ks[9], (L, N_HEADS_A, CHUNK, CHUNK), CHUNK ** -0.5),
        "a_spatial_b": 1.0 + nrm(ks[10], (L, N_HEADS_A, CHUNK), 0.1),
        "b_conv_w": nrm(ks[11], (L, CONV_WIDTH, D_B), CONV_WIDTH ** -0.5),
        "b_conv_b": nrm(ks[12], (L, D_B), 0.02),
        "b_gn_g": 1.0 + nrm(ks[13], (L, D_B), 0.05),
        "b_gn_b": nrm(ks[14], (L, D_B), 0.02),
        "out_norm_a_g": 1.0 + nrm(ks[15], (L, D_A), 0.05),
        "out_norm_b_g": 1.0 + nrm(ks[16], (L, D_B), 0.05),
        "w_out": nrm(ks[17], (L, D_MIX, D_MODEL), D_MIX ** -0.5),
        "norm2_g": 1.0 + nrm(ks[18], (L, D_MODEL), 0.05),
        "w_ffn_in": nrm(ks[19], (L, D_MODEL, 2 * D_FF), D_MODEL ** -0.5),
        "w_ffn_out": nrm(ks[20], (L, D_FF, D_MODEL), D_FF ** -0.5),
        "ada_f_w": nrm(ks[21], (D_MODEL, 2 * D_MODEL), D_MODEL ** -0.5),
        "ada_f_b": nrm(ks[22], (2 * D_MODEL,), 0.02),
        "norm_f_g": 1.0 + nrm(ks[23], (D_MODEL,), 0.05),
    }


def reference(x, c, ada_w, ada_b, norm1_g, w_in, b_in, a_ln_g, a_ln_b, a_spatial_w,
              a_spatial_b, b_conv_w, b_conv_b, b_gn_g, b_gn_b, out_norm_a_g, out_norm_b_g,
              w_out, norm2_g, w_ffn_in, w_ffn_out, ada_f_w, ada_f_b, norm_f_g):
    c_act = jax.nn.silu(c)
    for i in range(DEPTH):
        cond = c_act @ ada_w[i] + ada_b[i]
        shift1, scale1, gate1, shift2, scale2, gate2 = jnp.split(cond, N_MOD, axis=-1)

        h = modulate(rmsnorm(x, norm1_g[i]), shift1, scale1)
        z = h @ w_in[i] + b_in[i]
        u, v, val, gate = jnp.split(z, [D_A, 2 * D_A, 2 * D_A + D_B], axis=-1)
        y_a = spatial_gating_mixer(jax.nn.gelu(u, approximate=False),
                                   jax.nn.gelu(v, approximate=False),
                                   a_ln_g[i], a_ln_b[i], a_spatial_w[i], a_spatial_b[i])
        y_b = conformer_conv_mixer(val, gate, b_conv_w[i], b_conv_b[i], b_gn_g[i], b_gn_b[i])
        y = jnp.concatenate([rmsnorm(y_a, out_norm_a_g[i]), rmsnorm(y_b, out_norm_b_g[i])], axis=-1)
        x = x + gate1[:, None, :] * (y @ w_out[i])

        h2 = modulate(rmsnorm(x, norm2_g[i]), shift2, scale2)
        g_ff, up_ff = jnp.split(h2 @ w_ffn_in[i], 2, axis=-1)
        x = x + gate2[:, None, :] * ((jax.nn.silu(g_ff) * up_ff) @ w_ffn_out[i])

    shift_f, scale_f = jnp.split(c_act @ ada_f_w + ada_f_b, 2, axis=-1)
    return modulate(rmsnorm(x, norm_f_g), shift_f, scale_f)
```

```python
import functools

import jax
import jax.numpy as jnp
from jax import lax
from jax.experimental import pallas as pl
from jax.experimental.pallas import tpu as pltpu

D_MODEL = 1024
D_A = 512
D_B = 512
N_HEADS_A = 8
HEAD_DIM_A = D_A // N_HEADS_A
N_GROUPS_B = 8
GROUP_DIM_B = D_B // N_GROUPS_B
CHUNK = 128
CONV_WIDTH = 31
D_IN = 2 * D_A + 2 * D_B
D_FF = 2816
N_MOD = 6
EPS = 1e-6

LANES = 128
HALO = 32
COND_TN = 512
MIX_TS = 512
FFN_TM = 512
FFN_TF = 256
CONV_RB = 64
VMEM_LIMIT = 56 * 1024 * 1024

F32 = jnp.float32
BF16 = jnp.bfloat16


def _rms(x, g):
    ms = jnp.mean(x * x, axis=-1, keepdims=True)
    return x * lax.rsqrt(ms + EPS) * g


def _cond_kernel(ct_ref, w_ref, b_ref, o_ref):
    ct = ct_ref[...]
    act = ct * jax.nn.sigmoid(ct)
    w = w_ref[...]
    rows = [jnp.sum(w * act[:, b:b + 1], axis=0, keepdims=True) for b in range(ct.shape[1])]
    o_ref[...] = jnp.concatenate(rows, axis=0) + b_ref[...]


def _cond_linear(ct, w, b):
    d, n = w.shape
    nb = ct.shape[1]
    return pl.pallas_call(
        _cond_kernel,
        grid=(n // COND_TN,),
        in_specs=[
            pl.BlockSpec((d, nb), lambda j: (0, 0)),
            pl.BlockSpec((d, COND_TN), lambda j: (0, j)),
            pl.BlockSpec((1, COND_TN), lambda j: (0, j)),
        ],
        out_specs=pl.BlockSpec((nb, COND_TN), lambda j: (0, j)),
        out_shape=jax.ShapeDtypeStruct((nb, n), F32),
        compiler_params=pltpu.CompilerParams(dimension_semantics=("arbitrary",)),
        name="cond_linear",
    )(ct, w, b.reshape(1, n))


def _mixer_kernel(x_ref, mod_ref, g1_ref, w_in_ref, b_in_ref, lng_ref, lnb_ref, wsp_ref, bsp_ref,
                  cw_ref, cb_ref, gng_ref, gnb_ref, ona_ref, onb_ref, w_out_ref, o_ref, ypad_ref):
    ts = x_ref.shape[1]
    s_idx = pl.program_id(1)

    x = x_ref[0]
    mod = mod_ref[0]
    shift1, scale1, gate1 = mod[0:1], mod[1:2], mod[2:3]

    h = _rms(x, g1_ref[...]) * (1.0 + scale1) + shift1
    z = jnp.dot(h.astype(BF16), w_in_ref[...], preferred_element_type=F32) + b_in_ref[...]

    sqrt_half = 0.7071067811865476
    u = z[:, 0:D_A]
    u = 0.5 * u * (1.0 + lax.erf(u * sqrt_half))
    v = z[:, D_A:2 * D_A]
    v = 0.5 * v * (1.0 + lax.erf(v * sqrt_half))
    mu = jnp.mean(v, axis=-1, keepdims=True)
    vc = v - mu
    var = jnp.mean(vc * vc, axis=-1, keepdims=True)
    vn = vc * lax.rsqrt(var + EPS) * lng_ref[...] + lnb_ref[...]

    lane = lax.broadcasted_iota(jnp.int32, (1, D_A), 1)
    lo = (lane % LANES) < HEAD_DIM_A
    vn_lo = jnp.where(lo, vn, 0.0).astype(BF16)
    vn_hi = jnp.where(lo, 0.0, vn).astype(BF16)
    t_row = lax.broadcasted_iota(jnp.int32, (CHUNK, N_HEADS_A * CHUNK), 0)
    s_col = lax.broadcasted_iota(jnp.int32, (CHUNK, N_HEADS_A * CHUNK), 1) % CHUNK
    wsp = jnp.where(s_col <= t_row, wsp_ref[...], 0.0).astype(BF16)
    bsp = bsp_ref[...]
    mixed_chunks = []
    for c in range(ts // CHUNK):
        r0 = c * CHUNK
        cols = []
        for j in range(D_A // LANES):
            rhs = jnp.concatenate([vn_lo[r0:r0 + CHUNK, j * LANES:(j + 1) * LANES],
                                   vn_hi[r0:r0 + CHUNK, j * LANES:(j + 1) * LANES]], axis=0)
            cols.append(jnp.dot(wsp[:, 2 * j * CHUNK:(2 * j + 2) * CHUNK], rhs, preferred_element_type=F32))
        mixed_chunks.append(jnp.concatenate(cols, axis=1) + bsp)
    mixed = jnp.concatenate(mixed_chunks, axis=0)
    ya = _rms(u * mixed, ona_ref[...])

    y = z[:, 2 * D_A:2 * D_A + D_B] * jax.nn.sigmoid(z[:, 2 * D_A + D_B:])

    @pl.when(s_idx == 0)
    def _():
        ypad_ref[0:HALO, :] = jnp.zeros((HALO, D_B), F32)

    @pl.when(s_idx > 0)
    def _():
        ypad_ref[0:HALO, :] = ypad_ref[ts:ts + HALO, :]

    ypad_ref[HALO:HALO + ts, :] = y

    cw = cw_ref[...]
    cb = cb_ref[...]
    off = HALO - (CONV_WIDTH - 1)
    conv_rows = []
    for rb in range(ts // CONV_RB):
        r0 = rb * CONV_RB
        cols = []
        for j in range(D_B // LANES):
            cs = slice(j * LANES, (j + 1) * LANES)
            acc = jnp.broadcast_to(cb[:, cs], (CONV_RB, LANES))
            for k in range(CONV_WIDTH):
                acc = acc + cw[k:k + 1, cs] * ypad_ref[r0 + off + k:r0 + off + k + CONV_RB, cs]
            cols.append(acc)
        conv_rows.append(jnp.concatenate(cols, axis=1))
    yc = jnp.concatenate(conv_rows, axis=0)

    gi = lax.broadcasted_iota(jnp.int32, (D_B, D_B), 0) // GROUP_DIM_B
    gj = lax.broadcasted_iota(jnp.int32, (D_B, D_B), 1) // GROUP_DIM_B
    gavg = jnp.where(gi == gj, 1.0 / GROUP_DIM_B, 0.0).astype(BF16)
    yc_hi = yc.astype(BF16)
    yc_lo = (yc - yc_hi.astype(F32)).astype(BF16)
    gmean = (jnp.dot(yc_hi, gavg, preferred_element_type=F32)
             + jnp.dot(yc_lo, gavg, preferred_element_type=F32))
    ycc = yc - gmean
    sq = ycc * ycc
    sq_hi = sq.astype(BF16)
    sq_lo = (sq - sq_hi.astype(F32)).astype(BF16)
    gvar = (jnp.dot(sq_hi, gavg, preferred_element_type=F32)
            + jnp.dot(sq_lo, gavg, preferred_element_type=F32))
    yn = ycc * lax.rsqrt(gvar + EPS) * gng_ref[...] + gnb_ref[...]
    yb = _rms(yn * jax.nn.sigmoid(yn), onb_ref[...])

    proj = (jnp.dot(ya.astype(BF16), w_out_ref[0:D_A, :], preferred_element_type=F32)
            + jnp.dot(yb.astype(BF16), w_out_ref[D_A:, :], preferred_element_type=F32))
    o_ref[0] = x + gate1 * proj


def _mixer(x, mod, g1, w_in, b_in, lng, lnb, wsp, bsp, cw, cb, gng, gnb, ona, onb, w_out):
    bsz, seq, d = x.shape
    ts = MIX_TS
    const = lambda shape: pl.BlockSpec(shape, lambda b, s: (0,) * len(shape))
    return pl.pallas_call(
        _mixer_kernel,
        grid=(bsz, seq // ts),
        in_specs=[
            pl.BlockSpec((1, ts, d), lambda b, s: (b, s, 0)),
            pl.BlockSpec((1, N_MOD, d), lambda b, s: (b, 0, 0)),
            const((1, d)),
            const((d, D_IN)),
            const((1, D_IN)),
            const((1, D_A)),
            const((1, D_A)),
            const((CHUNK, N_HEADS_A * CHUNK)),
            const((CHUNK, D_A)),
            const((CONV_WIDTH, D_B)),
            const((1, D_B)),
            const((1, D_B)),
            const((1, D_B)),
            const((1, D_A)),
            const((1, D_B)),
            const((d, d)),
        ],
        out_specs=pl.BlockSpec((1, ts, d), lambda b, s: (b, s, 0)),
        out_shape=jax.ShapeDtypeStruct((bsz, seq, d), F32),
        scratch_shapes=[pltpu.VMEM((HALO + ts, D_B), F32)],
        compiler_params=pltpu.CompilerParams(
            dimension_semantics=("arbitrary", "arbitrary"), vmem_limit_bytes=VMEM_LIMIT),
        name="mixer",
    )(x, mod, g1, w_in, b_in, lng, lnb, wsp, bsp, cw, cb, gng, gnb, ona, onb, w_out)


def _ffn_kernel(x_ref, mod_ref, modf_ref, g2_ref, gf_ref, w1_ref, w2_ref, o_ref):
    x = x_ref[0]
    mod = mod_ref[0]
    shift2, scale2, gate2 = mod[3:4], mod[4:5], mod[5:6]
    modf = modf_ref[0]
    h = (_rms(x, g2_ref[...]) * (1.0 + scale2) + shift2).astype(BF16)
    acc = jnp.zeros(x.shape, F32)
    for f in range(D_FF // FFN_TF):
        g = jnp.dot(h, w1_ref[:, f * FFN_TF:(f + 1) * FFN_TF], preferred_element_type=F32)
        up = jnp.dot(h, w1_ref[:, D_FF + f * FFN_TF:D_FF + (f + 1) * FFN_TF], preferred_element_type=F32)
        a = (g * jax.nn.sigmoid(g) * up).astype(BF16)
        acc = acc + jnp.dot(a, w2_ref[f * FFN_TF:(f + 1) * FFN_TF, :], preferred_element_type=F32)
    x2 = x + gate2 * acc
    o_ref[0] = _rms(x2, gf_ref[...]) * (1.0 + modf[1:2]) + modf[0:1]


def _ffn(x, mod, modf, g2, gf, w1, w2):
    bsz, seq, d = x.shape
    tm = FFN_TM
    const = lambda shape: pl.BlockSpec(shape, lambda b, s: (0,) * len(shape))
    return pl.pallas_call(
        _ffn_kernel,
        grid=(bsz, seq // tm),
        in_specs=[
            pl.BlockSpec((1, tm, d), lambda b, s: (b, s, 0)),
            pl.BlockSpec((1, N_MOD, d), lambda b, s: (b, 0, 0)),
            pl.BlockSpec((1, 2, d), lambda b, s: (b, 0, 0)),
            const((1, d)),
            const((1, d)),
            const((d, 2 * D_FF)),
            const((D_FF, d)),
        ],
        out_specs=pl.BlockSpec((1, tm, d), lambda b, s: (b, s, 0)),
        out_shape=jax.ShapeDtypeStruct((bsz, seq, d), F32),
        compiler_params=pltpu.CompilerParams(
            dimension_semantics=("arbitrary", "arbitrary"), vmem_limit_bytes=VMEM_LIMIT),
        name="ffn",
    )(x, mod, modf, g2, gf, w1, w2)


def kernel(x, c, ada_w, ada_b, norm1_g, w_in, b_in, a_ln_g, a_ln_b, a_spatial_w, a_spatial_b, b_conv_w, b_conv_b, b_gn_g, b_gn_b, out_norm_a_g, out_norm_b_g, w_out, norm2_g, w_ffn_in, w_ffn_out, ada_f_w, ada_f_b, norm_f_g):
    bsz = x.shape[0]
    assert ada_w.shape[0] == 1, "single-layer block"
    ct = c.T
    row = lambda a: a.reshape(1, -1)
    mod = _cond_linear(ct, ada_w[0], ada_b[0]).reshape(bsz, N_MOD, D_MODEL)
    modf = _cond_linear(ct, ada_f_w, ada_f_b).reshape(bsz, 2, D_MODEL)
    wsp = a_spatial_w[0].transpose(1, 0, 2).reshape(CHUNK, N_HEADS_A * CHUNK)
    bsp = jnp.repeat(a_spatial_b[0].T, HEAD_DIM_A, axis=1)
    x1 = _mixer(x, mod, row(norm1_g[0]), w_in[0].astype(BF16), row(b_in[0]), row(a_ln_g[0]), row(a_ln_b[0]),
                wsp, bsp, b_conv_w[0], row(b_conv_b[0]), row(b_gn_g[0]), row(b_gn_b[0]),
                row(out_norm_a_g[0]), row(out_norm_b_g[0]), w_out[0].astype(BF16))
    return _ffn(x1, mod, modf, row(norm2_g[0]), row(norm_f_g),
                w_ffn_in[0].astype(BF16), w_ffn_out[0].astype(BF16))
```

```python
import jax
import jax.numpy as jnp
from jax import lax
from jax.experimental import pallas as pl
from jax.experimental.pallas import tpu as pltpu

D_MODEL = 1024
D_A = 512
D_B = 512
N_HEADS_A = 8
HEAD_DIM_A = D_A // N_HEADS_A
N_GROUPS_B = 8
GROUP_DIM_B = D_B // N_GROUPS_B
CHUNK = 128
CONV_WIDTH = 31
D_IN = 2 * D_A + 2 * D_B
D_FF = 2816
N_MOD = 6
EPS = 1e-6

LANES = 128
HALO = 32
COND_TN = 512
MIX_TS = 512
FFN_TM = 512
FFN_TF = 256
CONV_RB = 64
VMEM_LIMIT = 56 * 1024 * 1024

F32 = jnp.float32
BF16 = jnp.bfloat16


def _rms(x, g):
    ms = jnp.mean(x * x, axis=-1, keepdims=True)
    return x * lax.rsqrt(ms + EPS) * g


def _cond_kernel(ct_ref, w_ref, b_ref, o_ref):
    ct = ct_ref[...]
    act = ct * jax.nn.sigmoid(ct)
    w = w_ref[...]
    rows = [jnp.sum(w * act[:, b:b + 1], axis=0, keepdims=True) for b in range(ct.shape[1])]
    o_ref[...] = jnp.concatenate(rows, axis=0) + b_ref[...]


def _cond_linear(ct, w, b):
    d, n = w.shape
    nb = ct.shape[1]
    return pl.pallas_call(
        _cond_kernel,
        grid=(n // COND_TN,),
        in_specs=[
            pl.BlockSpec((d, nb), lambda j: (0, 0)),
            pl.BlockSpec((d, COND_TN), lambda j: (0, j)),
            pl.BlockSpec((1, COND_TN), lambda j: (0, j)),
        ],
        out_specs=pl.BlockSpec((nb, COND_TN), lambda j: (0, j)),
        out_shape=jax.ShapeDtypeStruct((nb, n), F32),
        compiler_params=pltpu.CompilerParams(dimension_semantics=("arbitrary",)),
        name="cond_linear",
    )(ct, w, b.reshape(1, n))


def _mixer_kernel(x_ref, mod_ref, g1_ref, w_in_ref, b_in_ref, lng_ref, lnb_ref, wsp_ref, bsp_ref,
                  cw_ref, cb_ref, gng_ref, gnb_ref, ona_ref, onb_ref, w_out_ref, o_ref, ypad_ref):
    ts = x_ref.shape[1]
    s_idx = pl.program_id(1)

    x = x_ref[0]
    mod = mod_ref[0]
    shift1, scale1, gate1 = mod[0:1], mod[1:2], mod[2:3]

    h = _rms(x, g1_ref[...]) * (1.0 + scale1) + shift1
    z = jnp.dot(h.astype(BF16), w_in_ref[...], preferred_element_type=F32) + b_in_ref[...]

    sqrt_half = 0.7071067811865476
    u = z[:, 0:D_A]
    u = 0.5 * u * (1.0 + lax.erf(u * sqrt_half))
    v = z[:, D_A:2 * D_A]
    v = 0.5 * v * (1.0 + lax.erf(v * sqrt_half))
    mu = jnp.mean(v, axis=-1, keepdims=True)
    vc = v - mu
    var = jnp.mean(vc * vc, axis=-1, keepdims=True)
    vn = vc * lax.rsqrt(var + EPS) * lng_ref[...] + lnb_ref[...]

    lane = lax.broadcasted_iota(jnp.int32, (1, D_A), 1)
    lo = (lane % LANES) < HEAD_DIM_A
    vn_lo = jnp.where(lo, vn, 0.0).astype(BF16)
    vn_hi = jnp.where(lo, 0.0, vn).astype(BF16)
    t_row = lax.broadcasted_iota(jnp.int32, (CHUNK, N_HEADS_A * CHUNK), 0)
    s_col = lax.broadcasted_iota(jnp.int32, (CHUNK, N_HEADS_A * CHUNK), 1) % CHUNK
    wsp = jnp.where(s_col <= t_row, wsp_ref[...], 0.0).astype(BF16)
    bsp = bsp_ref[...]
    mixed_chunks = []
    for c in range(ts // CHUNK):
        r0 = c * CHUNK
        cols = []
        for j in range(D_A // LANES):
            rhs = jnp.concatenate([vn_lo[r0:r0 + CHUNK, j * LANES:(j + 1) * LANES],
                                   vn_hi[r0:r0 + CHUNK, j * LANES:(j + 1) * LANES]], axis=0)
            cols.append(jnp.dot(wsp[:, 2 * j * CHUNK:(2 * j + 2) * CHUNK], rhs, preferred_element_type=F32))
        mixed_chunks.append(jnp.concatenate(cols, axis=1) + bsp)
    mixed = jnp.concatenate(mixed_chunks, axis=0)
    ya = _rms(u * mixed, ona_ref[...])

    y = z[:, 2 * D_A:2 * D_A + D_B] * jax.nn.sigmoid(z[:, 2 * D_A + D_B:])

    def rows(start, size):
        return pl.ds(2 * start, size, stride=2)

    @pl.when(s_idx == 0)
    def _():
        for j in range(D_B // LANES):
            ypad_ref[j, rows(0, HALO), :] = jnp.zeros((HALO, LANES), F32)

    @pl.when(s_idx > 0)
    def _():
        for j in range(D_B // LANES):
            ypad_ref[j, rows(0, HALO), :] = ypad_ref[j, rows(ts, HALO), :]

    for j in range(D_B // LANES):
        ypad_ref[j, rows(HALO, ts), :] = y[:, j * LANES:(j + 1) * LANES]

    cw = cw_ref[...]
    cb = cb_ref[...]
    off = HALO - (CONV_WIDTH - 1)
    conv_rows = []
    for rb in range(ts // CONV_RB):
        r0 = rb * CONV_RB
        cols = []
        for j in range(D_B // LANES):
            cs = slice(j * LANES, (j + 1) * LANES)
            acc = jnp.broadcast_to(cb[:, cs], (CONV_RB, LANES))
            for k in range(CONV_WIDTH):
                acc = acc + cw[k:k + 1, cs] * ypad_ref[j, rows(r0 + off + k, CONV_RB), :]
            cols.append(acc)
        conv_rows.append(jnp.concatenate(cols, axis=1))
    yc = jnp.concatenate(conv_rows, axis=0)

    gi = lax.broadcasted_iota(jnp.int32, (D_B, D_B), 0) // GROUP_DIM_B
    gj = lax.broadcasted_iota(jnp.int32, (D_B, D_B), 1) // GROUP_DIM_B
    gavg = jnp.where(gi == gj, 1.0 / GROUP_DIM_B, 0.0).astype(BF16)
    yc_hi = yc.astype(BF16)
    yc_lo = (yc - yc_hi.astype(F32)).astype(BF16)
    gmean = (jnp.dot(yc_hi, gavg, preferred_element_type=F32)
             + jnp.dot(yc_lo, gavg, preferred_element_type=F32))
    ycc = yc - gmean
    sq = ycc * ycc
    sq_hi = sq.astype(BF16)
    sq_lo = (sq - sq_hi.astype(F32)).astype(BF16)
    gvar = (jnp.dot(sq_hi, gavg, preferred_element_type=F32)
            + jnp.dot(sq_lo, gavg, preferred_element_type=F32))
    yn = ycc * lax.rsqrt(gvar + EPS) * gng_ref[...] + gnb_ref[...]
    yb = _rms(yn * jax.nn.sigmoid(yn), onb_ref[...])

    proj = (jnp.dot(ya.astype(BF16), w_out_ref[0:D_A, :], preferred_element_type=F32)
            + jnp.dot(yb.astype(BF16), w_out_ref[D_A:, :], preferred_element_type=F32))
    o_ref[0] = x + gate1 * proj


def _mixer(x, mod, g1, w_in, b_in, lng, lnb, wsp, bsp, cw, cb, gng, gnb, ona, onb, w_out):
    bsz, seq, d = x.shape
    ts = MIX_TS
    const = lambda shape: pl.BlockSpec(shape, lambda b, s: (0,) * len(shape))
    return pl.pallas_call(
        _mixer_kernel,
        grid=(bsz, seq // ts),
        in_specs=[
            pl.BlockSpec((1, ts, d), lambda b, s: (b, s, 0)),
            pl.BlockSpec((1, N_MOD, d), lambda b, s: (b, 0, 0)),
            const((1, d)),
            const((d, D_IN)),
            const((1, D_IN)),
            const((1, D_A)),
            const((1, D_A)),
            const((CHUNK, N_HEADS_A * CHUNK)),
            const((CHUNK, D_A)),
            const((CONV_WIDTH, D_B)),
            const((1, D_B)),
            const((1, D_B)),
            const((1, D_B)),
            const((1, D_A)),
            const((1, D_B)),
            const((d, d)),
        ],
        out_specs=pl.BlockSpec((1, ts, d), lambda b, s: (b, s, 0)),
        out_shape=jax.ShapeDtypeStruct((bsz, seq, d), F32),
        scratch_shapes=[pltpu.VMEM((D_B // LANES, 2 * (HALO + ts), LANES), F32)],
        compiler_params=pltpu.CompilerParams(
            dimension_semantics=("arbitrary", "arbitrary"), vmem_limit_bytes=VMEM_LIMIT),
        name="mixer",
    )(x, mod, g1, w_in, b_in, lng, lnb, wsp, bsp, cw, cb, gng, gnb, ona, onb, w_out)


def _ffn_kernel(x_ref, mod_ref, modf_ref, g2_ref, gf_ref, w1_ref, w2_ref, o_ref):
    x = x_ref[0]
    mod = mod_ref[0]
    shift2, scale2, gate2 = mod[3:4], mod[4:5], mod[5:6]
    modf = modf_ref[0]
    h = (_rms(x, g2_ref[...]) * (1.0 + scale2) + shift2).astype(BF16)
    acc = jnp.zeros(x.shape, F32)
    for f in range(D_FF // FFN_TF):
        g = jnp.dot(h, w1_ref[:, f * FFN_TF:(f + 1) * FFN_TF], preferred_element_type=F32)
        up = jnp.dot(h, w1_ref[:, D_FF + f * FFN_TF:D_FF + (f + 1) * FFN_TF], preferred_element_type=F32)
        a = (g * jax.nn.sigmoid(g) * up).astype(BF16)
        acc = acc + jnp.dot(a, w2_ref[f * FFN_TF:(f + 1) * FFN_TF, :], preferred_element_type=F32)
    x2 = x + gate2 * acc
    o_ref[0] = _rms(x2, gf_ref[...]) * (1.0 + modf[1:2]) + modf[0:1]


def _ffn(x, mod, modf, g2, gf, w1, w2):
    bsz, seq, d = x.shape
    tm = FFN_TM
    const = lambda shape: pl.BlockSpec(shape, lambda b, s: (0,) * len(shape))
    return pl.pallas_call(
        _ffn_kernel,
        grid=(bsz, seq // tm),
        in_specs=[
            pl.BlockSpec((1, tm, d), lambda b, s: (b, s, 0)),
            pl.BlockSpec((1, N_MOD, d), lambda b, s: (b, 0, 0)),
            pl.BlockSpec((1, 2, d), lambda b, s: (b, 0, 0)),
            const((1, d)),
            const((1, d)),
            const((d, 2 * D_FF)),
            const((D_FF, d)),
        ],
        out_specs=pl.BlockSpec((1, tm, d), lambda b, s: (b, s, 0)),
        out_shape=jax.ShapeDtypeStruct((bsz, seq, d), F32),
        compiler_params=pltpu.CompilerParams(
            dimension_semantics=("arbitrary", "arbitrary"), vmem_limit_bytes=VMEM_LIMIT),
        name="ffn",
    )(x, mod, modf, g2, gf, w1, w2)


def kernel(x, c, ada_w, ada_b, norm1_g, w_in, b_in, a_ln_g, a_ln_b, a_spatial_w, a_spatial_b, b_conv_w, b_conv_b, b_gn_g, b_gn_b, out_norm_a_g, out_norm_b_g, w_out, norm2_g, w_ffn_in, w_ffn_out, ada_f_w, ada_f_b, norm_f_g):
    bsz = x.shape[0]
    assert ada_w.shape[0] == 1, "single-layer block"
    ct = c.T
    row = lambda a: a.reshape(1, -1)
    mod = _cond_linear(ct, ada_w[0], ada_b[0]).reshape(bsz, N_MOD, D_MODEL)
    modf = _cond_linear(ct, ada_f_w, ada_f_b).reshape(bsz, 2, D_MODEL)
    wsp = a_spatial_w[0].transpose(1, 0, 2).reshape(CHUNK, N_HEADS_A * CHUNK)
    bsp = jnp.repeat(a_spatial_b[0].T, HEAD_DIM_A, axis=1)
    x1 = _mixer(x, mod, row(norm1_g[0]), w_in[0].astype(BF16), row(b_in[0]), row(a_ln_g[0]), row(a_ln_b[0]),
                wsp, bsp, b_conv_w[0], row(b_conv_b[0]), row(b_gn_g[0]), row(b_gn_b[0]),
                row(out_norm_a_g[0]), row(out_norm_b_g[0]), w_out[0].astype(BF16))
    return _ffn(x1, mod, modf, row(norm2_g[0]), row(norm_f_g),
                w_ffn_in[0].astype(BF16), w_ffn_out[0].astype(BF16))
```

```python
import functools

import jax
import jax.numpy as jnp
from jax import lax
from jax.experimental import pallas as pl
from jax.experimental.pallas import tpu as pltpu

D_MODEL = 1024
D_A = 512
D_B = 512
N_HEADS_A = 8
HEAD_DIM_A = D_A // N_HEADS_A
N_GROUPS_B = 8
GROUP_DIM_B = D_B // N_GROUPS_B
CHUNK = 128
CONV_WIDTH = 31
D_IN = 2 * D_A + 2 * D_B
D_FF = 2816
N_MOD = 6
EPS = 1e-6

LANES = 128
HALO = 32
COND_TN = 512
TILE = 512
FFN_TF = 256
N_FF = D_FF // FFN_TF
CONV_RB = 64
VMEM_LIMIT = 60 * 1024 * 1024

F32 = jnp.float32
BF16 = jnp.bfloat16


def _rms(x, g):
    ms = jnp.mean(x * x, axis=-1, keepdims=True)
    return x * lax.rsqrt(ms + EPS) * g


def _gelu(a):
    return 0.5 * a * (1.0 + lax.erf(a * 0.7071067811865476))


def _dot(a, b):
    return jnp.dot(a, b, preferred_element_type=F32)


def _cond_kernel(ct_ref, w_ref, b_ref, o_ref):
    ct = ct_ref[...]
    act = ct * jax.nn.sigmoid(ct)
    w = w_ref[...]
    rows = [jnp.sum(w * act[:, b:b + 1], axis=0, keepdims=True) for b in range(ct.shape[1])]
    o_ref[...] = jnp.concatenate(rows, axis=0) + b_ref[...]


def _cond_linear(ct, w, b):
    d, n = w.shape
    nb = ct.shape[1]
    return pl.pallas_call(
        _cond_kernel,
        grid=(n // COND_TN,),
        in_specs=[
            pl.BlockSpec((d, nb), lambda j: (0, 0)),
            pl.BlockSpec((d, COND_TN), lambda j: (0, j)),
            pl.BlockSpec((1, COND_TN), lambda j: (0, j)),
        ],
        out_specs=pl.BlockSpec((nb, COND_TN), lambda j: (0, j)),
        out_shape=jax.ShapeDtypeStruct((nb, n), F32),
        compiler_params=pltpu.CompilerParams(dimension_semantics=("arbitrary",)),
        name="cond_linear",
    )(ct, w, b.reshape(1, n))


def _conv_rows(start, size):
    return pl.ds(2 * start, size, stride=2)


class _Mixer:
    def __init__(self, x_ref, mod, g1_ref, w_in_ref, b_in_ref, lng_ref, lnb_ref, wsp_ref, bsp_ref,
                 cw_ref, cb_ref, gng_ref, gnb_ref, ona_ref, onb_ref, w_out_ref, ypad_ref):
        self.x_ref = x_ref
        self.mod = mod
        self.r = dict(g1=g1_ref, w_in=w_in_ref, b_in=b_in_ref, lng=lng_ref, lnb=lnb_ref, wsp=wsp_ref, bsp=bsp_ref,
                      cw=cw_ref, cb=cb_ref, gng=gng_ref, gnb=gnb_ref, ona=ona_ref, onb=onb_ref, w_out=w_out_ref)
        self.ypad_ref = ypad_ref
        self.ts = x_ref.shape[1]
        self.conv_out = []
        self.z = {}

    def norm(self):
        shift1, scale1 = self.mod[0:1], self.mod[1:2]
        h = _rms(self.x_ref[0], self.r["g1"][...]) * (1.0 + scale1) + shift1
        self.h = h.astype(BF16)

    def in_proj(self, q):
        cs = slice(q * D_A, (q + 1) * D_A)
        self.z[q] = _dot(self.h, self.r["w_in"][:, cs]) + self.r["b_in"][:, cs]

    def gate_u(self):
        self.u = _gelu(self.z.pop(0))

    def gate_v(self):
        v = _gelu(self.z.pop(1))
        mu = jnp.mean(v, axis=-1, keepdims=True)
        vc = v - mu
        var = jnp.mean(vc * vc, axis=-1, keepdims=True)
        vn = vc * lax.rsqrt(var + EPS) * self.r["lng"][...] + self.r["lnb"][...]
        lane = lax.broadcasted_iota(jnp.int32, (1, D_A), 1)
        lo = (lane % LANES) < HEAD_DIM_A
        self.vn_lo = jnp.where(lo, vn, 0.0).astype(BF16)
        self.vn_hi = jnp.where(lo, 0.0, vn).astype(BF16)

    def glu_b(self):
        ts = self.ts
        y = self.z.pop(2) * jax.nn.sigmoid(self.z.pop(3))
        for j in range(D_B // LANES):
            self.ypad_ref[j, _conv_rows(0, HALO), :] = self.ypad_ref[j, _conv_rows(ts, HALO), :]
        for j in range(D_B // LANES):
            self.ypad_ref[j, _conv_rows(HALO, ts), :] = y[:, j * LANES:(j + 1) * LANES]

    def spatial(self):
        t_row = lax.broadcasted_iota(jnp.int32, (CHUNK, N_HEADS_A * CHUNK), 0)
        s_col = lax.broadcasted_iota(jnp.int32, (CHUNK, N_HEADS_A * CHUNK), 1) % CHUNK
        wsp = jnp.where(s_col <= t_row, self.r["wsp"][...], 0.0).astype(BF16)
        chunks = []
        for c in range(self.ts // CHUNK):
            r0 = c * CHUNK
            cols = []
            for j in range(D_A // LANES):
                rhs = jnp.concatenate([self.vn_lo[r0:r0 + CHUNK, j * LANES:(j + 1) * LANES],
                                       self.vn_hi[r0:r0 + CHUNK, j * LANES:(j + 1) * LANES]], axis=0)
                cols.append(_dot(wsp[:, 2 * j * CHUNK:(2 * j + 2) * CHUNK], rhs))
            chunks.append(jnp.concatenate(cols, axis=1))
        self.mixed = chunks

    def mix_a(self):
        bsp = self.r["bsp"][...]
        mixed = jnp.concatenate([m + bsp for m in self.mixed], axis=0)
        self.ya = _rms(self.u * mixed, self.r["ona"][...]).astype(BF16)

    def conv(self, rb):
        cw = self.r["cw"][...]
        cb = self.r["cb"][...]
        off = HALO - (CONV_WIDTH - 1)
        r0 = rb * CONV_RB
        cols = []
        for j in range(D_B // LANES):
            cs = slice(j * LANES, (j + 1) * LANES)
            acc = jnp.broadcast_to(cb[:, cs], (CONV_RB, LANES))
            for k in range(CONV_WIDTH):
                acc = acc + cw[k:k + 1, cs] * self.ypad_ref[j, _conv_rows(r0 + off + k, CONV_RB), :]
            cols.append(acc)
        self.conv_out.append(jnp.concatenate(cols, axis=1))

    def split_mean(self):
        gi = lax.broadcasted_iota(jnp.int32, (D_B, D_B), 0) // GROUP_DIM_B
        gj = lax.broadcasted_iota(jnp.int32, (D_B, D_B), 1) // GROUP_DIM_B
        self.gavg = jnp.where(gi == gj, 1.0 / GROUP_DIM_B, 0.0).astype(BF16)
        self.yc = jnp.concatenate(self.conv_out, axis=0)
        self.yc_hi = self.yc.astype(BF16)
        self.yc_lo = (self.yc - self.yc_hi.astype(F32)).astype(BF16)

    def mean(self):
        self.gmean = _dot(self.yc_hi, self.gavg) + _dot(self.yc_lo, self.gavg)

    def split_var(self):
        self.ycc = self.yc - self.gmean
        sq = self.ycc * self.ycc
        self.sq_hi = sq.astype(BF16)
        self.sq_lo = (sq - self.sq_hi.astype(F32)).astype(BF16)

    def var(self):
        self.gvar = _dot(self.sq_hi, self.gavg) + _dot(self.sq_lo, self.gavg)

    def norm_b(self):
        yn = self.ycc * lax.rsqrt(self.gvar + EPS) * self.r["gng"][...] + self.r["gnb"][...]
        self.yb = _rms(yn * jax.nn.sigmoid(yn), self.r["onb"][...]).astype(BF16)

    def out_proj(self, x1_ref):
        w_out = self.r["w_out"]
        proj = _dot(self.ya, w_out[0:D_A, :]) + _dot(self.yb, w_out[D_A:, :])
        x1_ref[...] = self.x_ref[0] + self.mod[2:3] * proj


class _Ffn:
    def __init__(self, x_ref, mod, modo, g2_ref, gf_ref, w1_ref, w2_ref):
        self.x_ref, self.mod, self.modo = x_ref, mod, modo
        self.g2_ref, self.gf_ref, self.w1_ref, self.w2_ref = g2_ref, gf_ref, w1_ref, w2_ref
        self.gu = {}
        self.a = {}
        self.acc = None

    def norm(self):
        shift2, scale2 = self.mod[3:4], self.mod[4:5]
        self.x = self.x_ref[...]
        self.h = (_rms(self.x, self.g2_ref[...]) * (1.0 + scale2) + shift2).astype(BF16)

    def up(self, f):
        g = _dot(self.h, self.w1_ref[:, f * FFN_TF:(f + 1) * FFN_TF])
        up = _dot(self.h, self.w1_ref[:, D_FF + f * FFN_TF:D_FF + (f + 1) * FFN_TF])
        self.gu[f] = (g, up)

    def act(self, f):
        g, up = self.gu.pop(f)
        self.a[f] = (g * jax.nn.sigmoid(g) * up).astype(BF16)

    def down(self, f):
        d = _dot(self.a.pop(f), self.w2_ref[f * FFN_TF:(f + 1) * FFN_TF, :])
        self.acc = d if self.acc is None else self.acc + d

    def finish(self, o_ref):
        x2 = self.x + self.mod[5:6] * self.acc
        o_ref[0] = _rms(x2, self.gf_ref[...]) * (1.0 + self.modo[1:2]) + self.modo[0:1]


def _block_kernel(n_seq_tiles, x_ref, modm_ref, modf_ref, modo_ref, g1_ref, w_in_ref, b_in_ref, lng_ref, lnb_ref,
                  wsp_ref, bsp_ref, cw_ref, cb_ref, gng_ref, gnb_ref, ona_ref, onb_ref, w_out_ref,
                  g2_ref, gf_ref, w1_ref, w2_ref, o_ref, x1_ref, ypad_ref):
    t = pl.program_id(0)
    n_tiles = pl.num_programs(0) - 1
    ts = x_ref.shape[1]
    s_idx = lax.rem(jnp.minimum(t, n_tiles - 1), n_seq_tiles)

    @pl.when(t == 0)
    def _():
        x1_ref[...] = jnp.zeros(x1_ref.shape, F32)

    @pl.when(s_idx == 0)
    def _():
        for j in range(D_B // LANES):
            ypad_ref[j, _conv_rows(ts, HALO), :] = jnp.zeros((HALO, LANES), F32)

    mx = _Mixer(x_ref, modm_ref[0], g1_ref, w_in_ref, b_in_ref, lng_ref, lnb_ref, wsp_ref, bsp_ref,
                cw_ref, cb_ref, gng_ref, gnb_ref, ona_ref, onb_ref, w_out_ref, ypad_ref)
    ff = _Ffn(x1_ref, modf_ref[0], modo_ref[0], g2_ref, gf_ref, w1_ref, w2_ref)

    ff.norm(); mx.norm()
    ff.up(0)
    mx.in_proj(0)
    ff.up(1)
    mx.in_proj(1)
    ff.act(0)
    mx.gate_u()
    ff.down(0)
    mx.in_proj(2); mx.in_proj(3)
    ff.act(1); mx.gate_v()
    ff.down(1); ff.up(2); ff.up(3)
    mx.glu_b()
    ff.act(2); ff.act(3)
    mx.spatial()
    mx.conv(0)
    ff.down(2); ff.down(3); ff.up(4)
    mx.mix_a(); mx.conv(1)
    ff.act(4)
    ff.down(4); ff.up(5)
    mx.conv(2); mx.conv(3)
    ff.act(5)
    ff.down(5); ff.up(6)
    mx.conv(4); mx.conv(5)
    ff.act(6)
    ff.down(6); ff.up(7)
    mx.conv(6); mx.conv(7)
    ff.act(7)
    ff.down(7); ff.up(8)
    mx.split_mean()
    ff.act(8)
    ff.down(8)
    mx.mean()
    ff.up(9)
    mx.split_var()
    ff.act(9)
    ff.down(9)
    mx.var()
    ff.up(10)
    mx.norm_b()
    ff.act(10)
    ff.down(10)
    mx.out_proj(x1_ref)
    ff.finish(o_ref)


def _block(x, mod, modo, g1, w_in, b_in, lng, lnb, wsp, bsp, cw, cb, gng, gnb, ona, onb, w_out, g2, gf, w1, w2):
    bsz, seq, d = x.shape
    ts = TILE
    n_s = seq // ts
    n_tiles = bsz * n_s
    assert ts // CONV_RB == 8, "the stage order in _block_kernel is written for 8 conv row blocks"
    mix_tile = lambda t: jnp.minimum(t, n_tiles - 1)
    ffn_tile = lambda t: jnp.maximum(t - 1, 0)
    const = lambda shape: pl.BlockSpec(shape, lambda t: (0,) * len(shape))
    return pl.pallas_call(
        functools.partial(_block_kernel, n_s),
        grid=(n_tiles + 1,),
        in_specs=[
            pl.BlockSpec((1, ts, d), lambda t: (mix_tile(t) // n_s, mix_tile(t) % n_s, 0)),
            pl.BlockSpec((1, N_MOD, d), lambda t: (mix_tile(t) // n_s, 0, 0)),
            pl.BlockSpec((1, N_MOD, d), lambda t: (ffn_tile(t) // n_s, 0, 0)),
            pl.BlockSpec((1, 2, d), lambda t: (ffn_tile(t) // n_s, 0, 0)),
            const((1, d)),
            const((d, D_IN)),
            const((1, D_IN)),
            const((1, D_A)),
            const((1, D_A)),
            const((CHUNK, N_HEADS_A * CHUNK)),
            const((CHUNK, D_A)),
            const((CONV_WIDTH, D_B)),
            const((1, D_B)),
            const((1, D_B)),
            const((1, D_B)),
            const((1, D_A)),
            const((1, D_B)),
            const((d, d)),
            const((1, d)),
            const((1, d)),
            const((d, 2 * D_FF)),
            const((D_FF, d)),
        ],
        out_specs=pl.BlockSpec((1, ts, d), lambda t: (ffn_tile(t) // n_s, ffn_tile(t) % n_s, 0)),
        out_shape=jax.ShapeDtypeStruct((bsz, seq, d), F32),
        scratch_shapes=[pltpu.VMEM((ts, d), F32),
                        pltpu.VMEM((D_B // LANES, 2 * (HALO + ts), LANES), F32)],
        compiler_params=pltpu.CompilerParams(
            dimension_semantics=("arbitrary",), vmem_limit_bytes=VMEM_LIMIT),
        name="block",
    )(x, mod, mod, modo, g1, w_in, b_in, lng, lnb, wsp, bsp, cw, cb, gng, gnb, ona, onb, w_out, g2, gf, w1, w2)


def kernel(x, c, ada_w, ada_b, norm1_g, w_in, b_in, a_ln_g, a_ln_b, a_spatial_w, a_spatial_b, b_conv_w, b_conv_b, b_gn_g, b_gn_b, out_norm_a_g, out_norm_b_g, w_out, norm2_g, w_ffn_in, w_ffn_out, ada_f_w, ada_f_b, norm_f_g):
    bsz = x.shape[0]
    assert ada_w.shape[0] == 1, "single-layer block"
    ct = c.T
    row = lambda a: a.reshape(1, -1)
    mod = _cond_linear(ct, ada_w[0], ada_b[0]).reshape(bsz, N_MOD, D_MODEL)
    modo = _cond_linear(ct, ada_f_w, ada_f_b).reshape(bsz, 2, D_MODEL)
    wsp = a_spatial_w[0].transpose(1, 0, 2).reshape(CHUNK, N_HEADS_A * CHUNK)
    bsp = jnp.repeat(a_spatial_b[0].T, HEAD_DIM_A, axis=1)
    return _block(x, mod, modo, row(norm1_g[0]), w_in[0].astype(BF16), row(b_in[0]), row(a_ln_g[0]),
                  row(a_ln_b[0]), wsp, bsp, b_conv_w[0], row(b_conv_b[0]), row(b_gn_g[0]), row(b_gn_b[0]),
                  row(out_norm_a_g[0]), row(out_norm_b_g[0]), w_out[0].astype(BF16),
                  row(norm2_g[0]), row(norm_f_g), w_ffn_in[0].astype(BF16), w_ffn_out[0].astype(BF16))
```

```python
import functools

import jax
import jax.numpy as jnp
from jax import lax
from jax.experimental import pallas as pl
from jax.experimental.pallas import tpu as pltpu

D_MODEL = 1024
D_A = 512
D_B = 512
N_HEADS_A = 8
HEAD_DIM_A = D_A // N_HEADS_A
N_GROUPS_B = 8
GROUP_DIM_B = D_B // N_GROUPS_B
CHUNK = 128
CONV_WIDTH = 31
D_IN = 2 * D_A + 2 * D_B
D_FF = 2816
N_MOD = 6
EPS = 1e-6

LANES = 128
HALO = 32
COND_TN = 512
TILE = 512
FFN_TF = 256
N_FF = D_FF // FFN_TF
CONV_RB = 64
VMEM_LIMIT = 60 * 1024 * 1024

F32 = jnp.float32
BF16 = jnp.bfloat16


def _rms(x, g):
    ms = jnp.mean(x * x, axis=-1, keepdims=True)
    return x * lax.rsqrt(ms + EPS) * g


def _gelu(a):
    return 0.5 * a * (1.0 + lax.erf(a * 0.7071067811865476))


def _dot(a, b):
    return jnp.dot(a, b, preferred_element_type=F32)


def _cond_kernel(ct_ref, w_ref, b_ref, o_ref):
    ct = ct_ref[...]
    act = ct * jax.nn.sigmoid(ct)
    w = w_ref[...]
    rows = [jnp.sum(w * act[:, b:b + 1], axis=0, keepdims=True) for b in range(ct.shape[1])]
    o_ref[...] = jnp.concatenate(rows, axis=0) + b_ref[...]


def _cond_linear(ct, w, b):
    d, n = w.shape
    nb = ct.shape[1]
    return pl.pallas_call(
        _cond_kernel,
        grid=(n // COND_TN,),
        in_specs=[
            pl.BlockSpec((d, nb), lambda j: (0, 0)),
            pl.BlockSpec((d, COND_TN), lambda j: (0, j)),
            pl.BlockSpec((1, COND_TN), lambda j: (0, j)),
        ],
        out_specs=pl.BlockSpec((nb, COND_TN), lambda j: (0, j)),
        out_shape=jax.ShapeDtypeStruct((nb, n), F32),
        compiler_params=pltpu.CompilerParams(dimension_semantics=("arbitrary",)),
        name="cond_linear",
    )(ct, w, b.reshape(1, n))


def _conv_rows(start, size):
    return pl.ds(2 * start, size, stride=2)


class _Mixer:
    def __init__(self, x_ref, mod, g1_ref, w_in_ref, b_in_ref, lng_ref, lnb_ref, wsp_ref, bsp_ref,
                 cw_ref, cb_ref, gng_ref, gnb_ref, ona_ref, onb_ref, w_out_ref, ypad_ref):
        self.x_ref = x_ref
        self.mod = mod
        self.r = dict(g1=g1_ref, w_in=w_in_ref, b_in=b_in_ref, lng=lng_ref, lnb=lnb_ref, wsp=wsp_ref, bsp=bsp_ref,
                      cw=cw_ref, cb=cb_ref, gng=gng_ref, gnb=gnb_ref, ona=ona_ref, onb=onb_ref, w_out=w_out_ref)
        self.ypad_ref = ypad_ref
        self.ts = x_ref.shape[1]
        self.conv_out = []
        self.yb = []
        self.z = {}

    def norm(self):
        shift1, scale1 = self.mod[0:1], self.mod[1:2]
        h = _rms(self.x_ref[0], self.r["g1"][...]) * (1.0 + scale1) + shift1
        self.h = h.astype(BF16)

    def in_proj(self, q):
        cs = slice(q * D_A, (q + 1) * D_A)
        self.z[q] = _dot(self.h, self.r["w_in"][:, cs]) + self.r["b_in"][:, cs]

    def gate_u(self):
        self.u = _gelu(self.z.pop(0))

    def gate_v(self):
        v = _gelu(self.z.pop(1))
        mu = jnp.mean(v, axis=-1, keepdims=True)
        vc = v - mu
        var = jnp.mean(vc * vc, axis=-1, keepdims=True)
        vn = vc * lax.rsqrt(var + EPS) * self.r["lng"][...] + self.r["lnb"][...]
        lane = lax.broadcasted_iota(jnp.int32, (1, D_A), 1)
        lo = (lane % LANES) < HEAD_DIM_A
        self.vn_lo = jnp.where(lo, vn, 0.0).astype(BF16)
        self.vn_hi = jnp.where(lo, 0.0, vn).astype(BF16)

    def glu_b(self):
        ts = self.ts
        y = self.z.pop(2) * jax.nn.sigmoid(self.z.pop(3))
        for j in range(D_B // LANES):
            self.ypad_ref[j, _conv_rows(0, HALO), :] = self.ypad_ref[j, _conv_rows(ts, HALO), :]
        for j in range(D_B // LANES):
            self.ypad_ref[j, _conv_rows(HALO, ts), :] = y[:, j * LANES:(j + 1) * LANES]

    def spatial(self):
        t_row = lax.broadcasted_iota(jnp.int32, (CHUNK, N_HEADS_A * CHUNK), 0)
        s_col = lax.broadcasted_iota(jnp.int32, (CHUNK, N_HEADS_A * CHUNK), 1) % CHUNK
        wsp = jnp.where(s_col <= t_row, self.r["wsp"][...], 0.0).astype(BF16)
        n_chunks = self.ts // CHUNK
        per_pair = []
        for j in range(D_A // LANES):
            cs = slice(j * LANES, (j + 1) * LANES)
            rhs = jnp.concatenate(
                [jnp.concatenate([self.vn_lo[c * CHUNK:(c + 1) * CHUNK, cs],
                                  self.vn_hi[c * CHUNK:(c + 1) * CHUNK, cs]], axis=0) for c in range(n_chunks)],
                axis=1)
            per_pair.append(_dot(wsp[:, 2 * j * CHUNK:(2 * j + 2) * CHUNK], rhs))
        self.mixed = [jnp.concatenate([p[:, c * LANES:(c + 1) * LANES] for p in per_pair], axis=1)
                      for c in range(n_chunks)]

    def mix_a(self):
        bsp = self.r["bsp"][...]
        mixed = jnp.concatenate([m + bsp for m in self.mixed], axis=0)
        self.ya = _rms(self.u * mixed, self.r["ona"][...]).astype(BF16)

    def conv(self, rb):
        cw = self.r["cw"][...]
        cb = self.r["cb"][...]
        off = HALO - (CONV_WIDTH - 1)
        r0 = rb * CONV_RB
        cols = []
        for j in range(D_B // LANES):
            cs = slice(j * LANES, (j + 1) * LANES)
            acc = jnp.broadcast_to(cb[:, cs], (CONV_RB, LANES))
            for k in range(CONV_WIDTH):
                acc = acc + cw[k:k + 1, cs] * self.ypad_ref[j, _conv_rows(r0 + off + k, CONV_RB), :]
            cols.append(acc)
        self.conv_out.append(jnp.concatenate(cols, axis=1))

    @staticmethod
    def _group_mean(a):
        lane = lax.broadcasted_iota(jnp.int32, (1, LANES), 1)
        lo = lane < GROUP_DIM_B
        cols = []
        for j in range(D_B // LANES):
            aj = a[:, j * LANES:(j + 1) * LANES]
            s_lo = jnp.sum(jnp.where(lo, aj, 0.0), axis=-1, keepdims=True)
            s_hi = jnp.sum(jnp.where(lo, 0.0, aj), axis=-1, keepdims=True)
            cols.append(jnp.where(lo, s_lo, s_hi))
        return jnp.concatenate(cols, axis=1) * (1.0 / GROUP_DIM_B)

    def norm_b(self, rb):
        yc = self.conv_out[rb]
        ycc = yc - self._group_mean(yc)
        gvar = self._group_mean(ycc * ycc)
        yn = ycc * lax.rsqrt(gvar + EPS) * self.r["gng"][...] + self.r["gnb"][...]
        self.yb.append(_rms(yn * jax.nn.sigmoid(yn), self.r["onb"][...]).astype(BF16))

    def out_proj(self, x1_ref):
        w_out = self.r["w_out"]
        yb = jnp.concatenate(self.yb, axis=0)
        proj = _dot(self.ya, w_out[0:D_A, :]) + _dot(yb, w_out[D_A:, :])
        x1_ref[...] = self.x_ref[0] + self.mod[2:3] * proj


class _Ffn:
    def __init__(self, x_ref, mod, modo, g2_ref, gf_ref, w1_ref, w2_ref):
        self.x_ref, self.mod, self.modo = x_ref, mod, modo
        self.g2_ref, self.gf_ref, self.w1_ref, self.w2_ref = g2_ref, gf_ref, w1_ref, w2_ref
        self.gu = {}
        self.a = {}
        self.acc = None

    def norm(self):
        shift2, scale2 = self.mod[3:4], self.mod[4:5]
        self.x = self.x_ref[...]
        self.h = (_rms(self.x, self.g2_ref[...]) * (1.0 + scale2) + shift2).astype(BF16)

    def up(self, f):
        g = _dot(self.h, self.w1_ref[:, f * FFN_TF:(f + 1) * FFN_TF])
        up = _dot(self.h, self.w1_ref[:, D_FF + f * FFN_TF:D_FF + (f + 1) * FFN_TF])
        self.gu[f] = (g, up)

    def act(self, f):
        g, up = self.gu.pop(f)
        self.a[f] = (g * jax.nn.sigmoid(g) * up).astype(BF16)

    def down(self, f):
        d = _dot(self.a.pop(f), self.w2_ref[f * FFN_TF:(f + 1) * FFN_TF, :])
        self.acc = d if self.acc is None else self.acc + d

    def finish(self, o_ref):
        x2 = self.x + self.mod[5:6] * self.acc
        o_ref[0] = _rms(x2, self.gf_ref[...]) * (1.0 + self.modo[1:2]) + self.modo[0:1]


def _block_kernel(n_seq_tiles, x_ref, modm_ref, modf_ref, modo_ref, g1_ref, w_in_ref, b_in_ref, lng_ref, lnb_ref,
                  wsp_ref, bsp_ref, cw_ref, cb_ref, gng_ref, gnb_ref, ona_ref, onb_ref, w_out_ref,
                  g2_ref, gf_ref, w1_ref, w2_ref, o_ref, x1_ref, ypad_ref):
    t = pl.program_id(0)
    n_tiles = pl.num_programs(0) - 1
    ts = x_ref.shape[1]
    s_idx = lax.rem(jnp.minimum(t, n_tiles - 1), n_seq_tiles)

    @pl.when(t == 0)
    def _():
        x1_ref[...] = jnp.zeros(x1_ref.shape, F32)

    @pl.when(s_idx == 0)
    def _():
        for j in range(D_B // LANES):
            ypad_ref[j, _conv_rows(ts, HALO), :] = jnp.zeros((HALO, LANES), F32)

    mx = _Mixer(x_ref, modm_ref[0], g1_ref, w_in_ref, b_in_ref, lng_ref, lnb_ref, wsp_ref, bsp_ref,
                cw_ref, cb_ref, gng_ref, gnb_ref, ona_ref, onb_ref, w_out_ref, ypad_ref)
    ff = _Ffn(x1_ref, modf_ref[0], modo_ref[0], g2_ref, gf_ref, w1_ref, w2_ref)

    ff.norm(); mx.norm()
    ff.up(0)
    mx.in_proj(2); mx.in_proj(3)
    ff.up(1)
    mx.in_proj(1)
    ff.act(0); mx.glu_b()
    ff.down(0)
    mx.in_proj(0)
    ff.act(1); mx.gate_v()
    ff.down(1); ff.up(2)
    mx.conv(0); mx.gate_u()
    ff.act(2)
    mx.spatial()
    ff.down(2); ff.up(3)
    mx.norm_b(0); mx.conv(1); mx.mix_a()
    ff.act(3)
    for f in range(3, N_FF - 1):
        ff.down(f); ff.up(f + 1)
        mx.norm_b(f - 2)
        if f - 1 < ts // CONV_RB:
            mx.conv(f - 1)
        ff.act(f + 1)
    ff.down(N_FF - 1)
    mx.out_proj(x1_ref)
    ff.finish(o_ref)


def _block(x, mod, modo, g1, w_in, b_in, lng, lnb, wsp, bsp, cw, cb, gng, gnb, ona, onb, w_out, g2, gf, w1, w2):
    bsz, seq, d = x.shape
    ts = TILE
    n_s = seq // ts
    n_tiles = bsz * n_s
    assert ts // CONV_RB == 8, "the stage order in _block_kernel is written for 8 conv row blocks"
    mix_tile = lambda t: jnp.minimum(t, n_tiles - 1)
    ffn_tile = lambda t: jnp.maximum(t - 1, 0)
    const = lambda shape: pl.BlockSpec(shape, lambda t: (0,) * len(shape))
    return pl.pallas_call(
        functools.partial(_block_kernel, n_s),
        grid=(n_tiles + 1,),
        in_specs=[
            pl.BlockSpec((1, ts, d), lambda t: (mix_tile(t) // n_s, mix_tile(t) % n_s, 0)),
            pl.BlockSpec((1, N_MOD, d), lambda t: (mix_tile(t) // n_s, 0, 0)),
            pl.BlockSpec((1, N_MOD, d), lambda t: (ffn_tile(t) // n_s, 0, 0)),
            pl.BlockSpec((1, 2, d), lambda t: (ffn_tile(t) // n_s, 0, 0)),
            const((1, d)),
            const((d, D_IN)),
            const((1, D_IN)),
            const((1, D_A)),
            const((1, D_A)),
            const((CHUNK, N_HEADS_A * CHUNK)),
            const((CHUNK, D_A)),
            const((CONV_WIDTH, D_B)),
            const((1, D_B)),
            const((1, D_B)),
            const((1, D_B)),
            const((1, D_A)),
            const((1, D_B)),
            const((d, d)),
            const((1, d)),
            const((1, d)),
            const((d, 2 * D_FF)),
            const((D_FF, d)),
        ],
        out_specs=pl.BlockSpec((1, ts, d), lambda t: (ffn_tile(t) // n_s, ffn_tile(t) % n_s, 0)),
        out_shape=jax.ShapeDtypeStruct((bsz, seq, d), F32),
        scratch_shapes=[pltpu.VMEM((ts, d), F32),
                        pltpu.VMEM((D_B // LANES, 2 * (HALO + ts), LANES), F32)],
        compiler_params=pltpu.CompilerParams(
            dimension_semantics=("arbitrary",), vmem_limit_bytes=VMEM_LIMIT),
        name="block",
    )(x, mod, mod, modo, g1, w_in, b_in, lng, lnb, wsp, bsp, cw, cb, gng, gnb, ona, onb, w_out, g2, gf, w1, w2)


def kernel(x, c, ada_w, ada_b, norm1_g, w_in, b_in, a_ln_g, a_ln_b, a_spatial_w, a_spatial_b, b_conv_w, b_conv_b, b_gn_g, b_gn_b, out_norm_a_g, out_norm_b_g, w_out, norm2_g, w_ffn_in, w_ffn_out, ada_f_w, ada_f_b, norm_f_g):
    bsz = x.shape[0]
    assert ada_w.shape[0] == 1, "single-layer block"
    ct = c.T
    row = lambda a: a.reshape(1, -1)
    mod = _cond_linear(ct, ada_w[0], ada_b[0]).reshape(bsz, N_MOD, D_MODEL)
    modo = _cond_linear(ct, ada_f_w, ada_f_b).reshape(bsz, 2, D_MODEL)
    wsp = a_spatial_w[0].transpose(1, 0, 2).reshape(CHUNK, N_HEADS_A * CHUNK)
    bsp = jnp.repeat(a_spatial_b[0].T, HEAD_DIM_A, axis=1)
    return _block(x, mod, modo, row(norm1_g[0]), w_in[0].astype(BF16), row(b_in[0]), row(a_ln_g[0]),
                  row(a_ln_b[0]), wsp, bsp, b_conv_w[0], row(b_conv_b[0]), row(b_gn_g[0]), row(b_gn_b[0]),
                  row(out_norm_a_g[0]), row(out_norm_b_g[0]), w_out[0].astype(BF16),
                  row(norm2_g[0]), row(norm_f_g), w_ffn_in[0].astype(BF16), w_ffn_out[0].astype(BF16))
```

```python
import functools

import jax
import jax.numpy as jnp
from jax import lax
from jax.experimental import pallas as pl
from jax.experimental.pallas import tpu as pltpu

D_MODEL = 1024
D_A = 512
D_B = 512
N_HEADS_A = 8
HEAD_DIM_A = D_A // N_HEADS_A
N_GROUPS_B = 8
GROUP_DIM_B = D_B // N_GROUPS_B
CHUNK = 128
CONV_WIDTH = 31
D_IN = 2 * D_A + 2 * D_B
D_FF = 2816
N_MOD = 6
EPS = 1e-6

LANES = 128
SUBLANES = 8
HALO = 32
COND_TN = 1024
TILE = 512
FFN_TF = 256
N_FF = D_FF // FFN_TF
CONV_RB = 64
GLU_ROWS = 256
VMEM_LIMIT = 60 * 1024 * 1024

F32 = jnp.float32
BF16 = jnp.bfloat16


def _rms(x, g):
    ms = jnp.mean(x * x, axis=-1, keepdims=True)
    return x * lax.rsqrt(ms + EPS) * g


def _gelu(a):
    return 0.5 * a * (1.0 + lax.erf(a * 0.7071067811865476))


def _dot(a, b):
    return jnp.dot(a, b, preferred_element_type=F32)


def _cond_kernel(ct_ref, w_ref, b_ref, o_ref):
    ct = ct_ref[...]
    act = ct * jax.nn.sigmoid(ct)
    w = w_ref[...]
    rows = [jnp.sum(w * act[:, b:b + 1], axis=0, keepdims=True) for b in range(ct.shape[1])]
    o_ref[...] = jnp.concatenate(rows, axis=0) + b_ref[...]


def _cond_linear(ct, w, b):
    d, n = w.shape
    nb = ct.shape[1]
    return pl.pallas_call(
        _cond_kernel,
        grid=(n // COND_TN,),
        in_specs=[
            pl.BlockSpec((d, nb), lambda j: (0, 0)),
            pl.BlockSpec((d, COND_TN), lambda j: (0, j)),
            pl.BlockSpec((1, COND_TN), lambda j: (0, j)),
        ],
        out_specs=pl.BlockSpec((nb, COND_TN), lambda j: (0, j)),
        out_shape=jax.ShapeDtypeStruct((nb, n), F32),
        compiler_params=pltpu.CompilerParams(dimension_semantics=("arbitrary",), vmem_limit_bytes=VMEM_LIMIT),
        name="cond_linear",
    )(ct, w, b.reshape(1, n))


def _conv_rows(start, size):
    return pl.ds(2 * start, size, stride=2)


class _Mixer:
    def __init__(self, x_ref, mod, g1_ref, w_in_ref, b_in_ref, lng_ref, lnb_ref, wsp_ref, bsp_ref,
                 cw_ref, cb_ref, gng_ref, gnb_ref, ona_ref, onb_ref, w_out_ref, ypad_ref):
        self.x_ref = x_ref
        self.mod = mod
        self.r = dict(g1=g1_ref, w_in=w_in_ref, b_in=b_in_ref, lng=lng_ref, lnb=lnb_ref, wsp=wsp_ref, bsp=bsp_ref,
                      cw=cw_ref, cb=cb_ref, gng=gng_ref, gnb=gnb_ref, ona=ona_ref, onb=onb_ref, w_out=w_out_ref)
        self.ypad_ref = ypad_ref
        self.ts = x_ref.shape[1]
        self.conv_out = {}
        self.yb = {}
        self.z = {}
        self.zb = {}

    def norm(self):
        shift1, scale1 = self.mod[0:1], self.mod[1:2]
        h = _rms(self.x_ref[0], self.r["g1"][...]) * (1.0 + scale1) + shift1
        self.h = h.astype(BF16)

    def in_proj(self, q):
        cs = slice(q * D_A, (q + 1) * D_A)
        self.z[q] = _dot(self.h, self.r["w_in"][:, cs]) + self.r["b_in"][:, cs]

    def in_glu(self, g):
        rs = slice(g * GLU_ROWS, (g + 1) * GLU_ROWS)
        cs = slice(2 * D_A, D_IN)
        self.zb[g] = _dot(self.h[rs], self.r["w_in"][:, cs]) + self.r["b_in"][:, cs]

    def gate_u(self):
        self.u = _gelu(self.z.pop(0))

    def gate_v(self):
        v = _gelu(self.z.pop(1))
        mu = jnp.mean(v, axis=-1, keepdims=True)
        vc = v - mu
        var = jnp.mean(vc * vc, axis=-1, keepdims=True)
        vn = vc * lax.rsqrt(var + EPS) * self.r["lng"][...] + self.r["lnb"][...]
        lane = lax.broadcasted_iota(jnp.int32, (1, D_A), 1)
        lo = (lane % LANES) < HEAD_DIM_A
        self.vn_lo = jnp.where(lo, vn, 0.0).astype(BF16)
        self.vn_hi = jnp.where(lo, 0.0, vn).astype(BF16)

    def glu_b(self, g):
        ts = self.ts
        zb = self.zb.pop(g)
        y = zb[:, 0:D_B] * jax.nn.sigmoid(zb[:, D_B:])
        if g == 0:
            for j in range(D_B // LANES):
                self.ypad_ref[j, _conv_rows(0, HALO), :] = self.ypad_ref[j, _conv_rows(ts, HALO), :]
        for j in range(D_B // LANES):
            self.ypad_ref[j, _conv_rows(HALO + g * GLU_ROWS, GLU_ROWS), :] = y[:, j * LANES:(j + 1) * LANES]

    def spatial(self):
        t_row = lax.broadcasted_iota(jnp.int32, (CHUNK, N_HEADS_A * CHUNK), 0)
        s_col = lax.broadcasted_iota(jnp.int32, (CHUNK, N_HEADS_A * CHUNK), 1) % CHUNK
        wsp = jnp.where(s_col <= t_row, self.r["wsp"][...], 0.0).astype(BF16)
        n_chunks = self.ts // CHUNK
        per_pair = []
        for j in range(D_A // LANES):
            cs = slice(j * LANES, (j + 1) * LANES)
            rhs = jnp.concatenate(
                [jnp.concatenate([self.vn_lo[c * CHUNK:(c + 1) * CHUNK, cs],
                                  self.vn_hi[c * CHUNK:(c + 1) * CHUNK, cs]], axis=0) for c in range(n_chunks)],
                axis=1)
            per_pair.append(_dot(wsp[:, 2 * j * CHUNK:(2 * j + 2) * CHUNK], rhs))
        self.mixed = [jnp.concatenate([p[:, c * LANES:(c + 1) * LANES] for p in per_pair], axis=1)
                      for c in range(n_chunks)]

    def mix_a(self):
        bsp = self.r["bsp"][...]
        mixed = jnp.concatenate([m + bsp for m in self.mixed], axis=0)
        self.ya = _rms(self.u * mixed, self.r["ona"][...]).astype(BF16)

    def conv(self, rb):
        cw = self.r["cw"][...]
        cb = self.r["cb"][...]
        off = HALO - (CONV_WIDTH - 1)
        r0 = rb * CONV_RB
        cols = []
        for j in range(D_B // LANES):
            cs = slice(j * LANES, (j + 1) * LANES)
            acc = jnp.broadcast_to(cb[:, cs], (CONV_RB, LANES))
            for rho in range(min(SUBLANES, CONV_WIDTH)):
                taps = range(rho, CONV_WIDTH, SUBLANES)
                win = self.ypad_ref[j, _conv_rows(r0 + off + rho, CONV_RB + SUBLANES * (len(taps) - 1)), :]
                for q, k in enumerate(taps):
                    acc = acc + cw[k:k + 1, cs] * win[SUBLANES * q:SUBLANES * q + CONV_RB]
            cols.append(acc)
        self.conv_out[rb] = jnp.concatenate(cols, axis=1)

    @staticmethod
    def _group_mean(a):
        lane = lax.broadcasted_iota(jnp.int32, (1, LANES), 1)
        lo = lane < GROUP_DIM_B
        cols = []
        for j in range(D_B // LANES):
            aj = a[:, j * LANES:(j + 1) * LANES]
            s_lo = jnp.sum(jnp.where(lo, aj, 0.0), axis=-1, keepdims=True)
            s_hi = jnp.sum(jnp.where(lo, 0.0, aj), axis=-1, keepdims=True)
            cols.append(jnp.where(lo, s_lo, s_hi))
        return jnp.concatenate(cols, axis=1) * (1.0 / GROUP_DIM_B)

    def norm_b(self, rb):
        yc = self.conv_out.pop(rb)
        ycc = yc - self._group_mean(yc)
        gvar = self._group_mean(ycc * ycc)
        yn = ycc * lax.rsqrt(gvar + EPS) * self.r["gng"][...] + self.r["gnb"][...]
        self.yb[rb] = _rms(yn * jax.nn.sigmoid(yn), self.r["onb"][...]).astype(BF16)

    def out_proj(self, x1_ref):
        w_out = self.r["w_out"]
        yb = jnp.concatenate([self.yb[rb] for rb in range(self.ts // CONV_RB)], axis=0)
        proj = _dot(self.ya, w_out[0:D_A, :]) + _dot(yb, w_out[D_A:, :])
        x1_ref[...] = self.x_ref[0] + self.mod[2:3] * proj


class _Ffn:
    def __init__(self, x_ref, mod, modo, g2_ref, gf_ref, w1_ref, w2_ref):
        self.x_ref, self.mod, self.modo = x_ref, mod, modo
        self.g2_ref, self.gf_ref, self.w1_ref, self.w2_ref = g2_ref, gf_ref, w1_ref, w2_ref
        self.gu = {}
        self.a = {}

    def norm(self):
        shift2, scale2 = self.mod[3:4], self.mod[4:5]
        self.x = self.x_ref[...]
        self.h = (_rms(self.x, self.g2_ref[...]) * (1.0 + scale2) + shift2).astype(BF16)

    def up(self, f):
        g = _dot(self.h, self.w1_ref[:, f * FFN_TF:(f + 1) * FFN_TF])
        up = _dot(self.h, self.w1_ref[:, D_FF + f * FFN_TF:D_FF + (f + 1) * FFN_TF])
        self.gu[f] = (g, up)

    def act(self, f):
        g, up = self.gu.pop(f)
        self.a[f] = (g * jax.nn.sigmoid(g) * up).astype(BF16)

    def down(self):
        a = jnp.concatenate([self.a.pop(f) for f in range(N_FF)], axis=1)
        self.acc = _dot(a, self.w2_ref[...])

    def finish(self, o_ref):
        x2 = self.x + self.mod[5:6] * self.acc
        o_ref[0] = _rms(x2, self.gf_ref[...]) * (1.0 + self.modo[1:2]) + self.modo[0:1]


def _block_kernel(n_seq_tiles, x_ref, modm_ref, modf_ref, modo_ref, g1_ref, w_in_ref, b_in_ref, lng_ref, lnb_ref,
                  wsp_ref, bsp_ref, cw_ref, cb_ref, gng_ref, gnb_ref, ona_ref, onb_ref, w_out_ref,
                  g2_ref, gf_ref, w1_ref, w2_ref, o_ref, x1_ref, ypad_ref):
    t = pl.program_id(0)
    n_tiles = pl.num_programs(0) - 1
    ts = x_ref.shape[1]
    s_idx = lax.rem(jnp.minimum(t, n_tiles - 1), n_seq_tiles)

    @pl.when(t == 0)
    def _():
        x1_ref[...] = jnp.zeros(x1_ref.shape, F32)

    @pl.when(s_idx == 0)
    def _():
        for j in range(D_B // LANES):
            ypad_ref[j, _conv_rows(ts, HALO), :] = jnp.zeros((HALO, LANES), F32)

    mx = _Mixer(x_ref, modm_ref[0], g1_ref, w_in_ref, b_in_ref, lng_ref, lnb_ref, wsp_ref, bsp_ref,
                cw_ref, cb_ref, gng_ref, gnb_ref, ona_ref, onb_ref, w_out_ref, ypad_ref)
    ff = _Ffn(x1_ref, modf_ref[0], modo_ref[0], g2_ref, gf_ref, w1_ref, w2_ref)

    def conv_group(g):
        for rb in range(g * GLU_ROWS // CONV_RB, (g + 1) * GLU_ROWS // CONV_RB):
            mx.conv(rb)
            mx.norm_b(rb)

    ff.norm(); mx.norm()
    ff.up(0)
    mx.in_glu(0); mx.glu_b(0)
    ff.up(1); ff.act(0)
    mx.in_proj(1)
    ff.up(2); ff.act(1)
    mx.gate_v()
    mx.in_proj(0)
    ff.up(3); ff.act(2)
    conv_group(0)
    mx.gate_u()
    mx.spatial()
    ff.up(4); ff.act(3)
    mx.mix_a()
    ff.up(5); ff.act(4)
    mx.in_glu(1); mx.glu_b(1)
    ff.up(6); ff.act(5)
    ff.up(7); ff.act(6)
    ff.up(8); ff.act(7)
    conv_group(1)
    ff.up(9); ff.act(8)
    ff.up(10); ff.act(9); ff.act(10)
    ff.down()
    mx.out_proj(x1_ref)
    ff.finish(o_ref)


def _block(x, mod, modo, g1, w_in, b_in, lng, lnb, wsp, bsp, cw, cb, gng, gnb, ona, onb, w_out, g2, gf, w1, w2):
    bsz, seq, d = x.shape
    ts = TILE
    n_s = seq // ts
    n_tiles = bsz * n_s
    assert ts // GLU_ROWS == 2 and N_FF == 11, "the stage order in _block_kernel is written for these counts"
    assert 2 * GROUP_DIM_B == LANES and 2 * HEAD_DIM_A == LANES, "two groups / heads per 128-lane column"
    mix_tile = lambda t: jnp.minimum(t, n_tiles - 1)
    ffn_tile = lambda t: jnp.maximum(t - 1, 0)
    const = lambda shape: pl.BlockSpec(shape, lambda t: (0,) * len(shape))
    return pl.pallas_call(
        functools.partial(_block_kernel, n_s),
        grid=(n_tiles + 1,),
        in_specs=[
            pl.BlockSpec((1, ts, d), lambda t: (mix_tile(t) // n_s, mix_tile(t) % n_s, 0)),
            pl.BlockSpec((1, N_MOD, d), lambda t: (mix_tile(t) // n_s, 0, 0)),
            pl.BlockSpec((1, N_MOD, d), lambda t: (ffn_tile(t) // n_s, 0, 0)),
            pl.BlockSpec((1, 2, d), lambda t: (ffn_tile(t) // n_s, 0, 0)),
            const((1, d)),
            const((d, D_IN)),
            const((1, D_IN)),
            const((1, D_A)),
            const((1, D_A)),
            const((CHUNK, N_HEADS_A * CHUNK)),
            const((CHUNK, D_A)),
            const((CONV_WIDTH, D_B)),
            const((1, D_B)),
            const((1, D_B)),
            const((1, D_B)),
            const((1, D_A)),
            const((1, D_B)),
            const((d, d)),
            const((1, d)),
            const((1, d)),
            const((d, 2 * D_FF)),
            const((D_FF, d)),
        ],
        out_specs=pl.BlockSpec((1, ts, d), lambda t: (ffn_tile(t) // n_s, ffn_tile(t) % n_s, 0)),
        out_shape=jax.ShapeDtypeStruct((bsz, seq, d), F32),
        scratch_shapes=[pltpu.VMEM((ts, d), F32),
                        pltpu.VMEM((D_B // LANES, 2 * (HALO + ts), LANES), F32)],
        compiler_params=pltpu.CompilerParams(
            dimension_semantics=("arbitrary",), vmem_limit_bytes=VMEM_LIMIT),
        name="block",
    )(x, mod, mod, modo, g1, w_in, b_in, lng, lnb, wsp, bsp, cw, cb, gng, gnb, ona, onb, w_out, g2, gf, w1, w2)


def kernel(x, c, ada_w, ada_b, norm1_g, w_in, b_in, a_ln_g, a_ln_b, a_spatial_w, a_spatial_b, b_conv_w, b_conv_b, b_gn_g, b_gn_b, out_norm_a_g, out_norm_b_g, w_out, norm2_g, w_ffn_in, w_ffn_out, ada_f_w, ada_f_b, norm_f_g):
    bsz = x.shape[0]
    assert ada_w.shape[0] == 1, "single-layer block"
    ct = c.T
    row = lambda a: a.reshape(1, -1)
    mod = _cond_linear(ct, ada_w[0], ada_b[0]).reshape(bsz, N_MOD, D_MODEL)
    modo = _cond_linear(ct, ada_f_w, ada_f_b).reshape(bsz, 2, D_MODEL)
    wsp = a_spatial_w[0].transpose(1, 0, 2).reshape(CHUNK, N_HEADS_A * CHUNK)
    bsp = jnp.repeat(a_spatial_b[0].T, HEAD_DIM_A, axis=1)
    return _block(x, mod, modo, row(norm1_g[0]), w_in[0].astype(BF16), row(b_in[0]), row(a_ln_g[0]),
                  row(a_ln_b[0]), wsp, bsp, b_conv_w[0], row(b_conv_b[0]), row(b_gn_g[0]), row(b_gn_b[0]),
                  row(out_norm_a_g[0]), row(out_norm_b_g[0]), w_out[0].astype(BF16),
                  row(norm2_g[0]), row(norm_f_g), w_ffn_in[0].astype(BF16), w_ffn_out[0].astype(BF16))
```

```python
import functools

import jax
import jax.numpy as jnp
from jax import lax
from jax.experimental import pallas as pl
from jax.experimental.pallas import tpu as pltpu

D_MODEL = 1024
D_A = 512
D_B = 512
N_HEADS_A = 8
HEAD_DIM_A = D_A // N_HEADS_A
N_GROUPS_B = 8
GROUP_DIM_B = D_B // N_GROUPS_B
CHUNK = 128
CONV_WIDTH = 31
D_IN = 2 * D_A + 2 * D_B
D_FF = 2816
N_MOD = 6
EPS = 1e-6

LANES = 128
SUBLANES = 8
HALO = 32
COND_TN = 1024
TILE = 512
FFN_TF = 256
N_FF = D_FF // FFN_TF
CONV_RB = 64
CONV_UNIT = 16
GLU_ROWS = 256
VMEM_LIMIT = 60 * 1024 * 1024

F32 = jnp.float32
BF16 = jnp.bfloat16


def _rms(x, g):
    ms = jnp.mean(x * x, axis=-1, keepdims=True)
    return x * lax.rsqrt(ms + EPS) * g


def _gelu(a):
    return 0.5 * a * (1.0 + lax.erf(a * 0.7071067811865476))


def _dot(a, b):
    return jnp.dot(a, b, preferred_element_type=F32)


def _cond_kernel(ct_ref, w_ref, b_ref, o_ref):
    ct = ct_ref[...]
    act = ct * jax.nn.sigmoid(ct)
    w = w_ref[...]
    rows = [jnp.sum(w * act[:, b:b + 1], axis=0, keepdims=True) for b in range(ct.shape[1])]
    o_ref[...] = jnp.concatenate(rows, axis=0) + b_ref[...]


def _cond_linear(ct, w, b):
    d, n = w.shape
    nb = ct.shape[1]
    return pl.pallas_call(
        _cond_kernel,
        grid=(n // COND_TN,),
        in_specs=[
            pl.BlockSpec((d, nb), lambda j: (0, 0)),
            pl.BlockSpec((d, COND_TN), lambda j: (0, j)),
            pl.BlockSpec((1, COND_TN), lambda j: (0, j)),
        ],
        out_specs=pl.BlockSpec((nb, COND_TN), lambda j: (0, j)),
        out_shape=jax.ShapeDtypeStruct((nb, n), F32),
        compiler_params=pltpu.CompilerParams(dimension_semantics=("arbitrary",), vmem_limit_bytes=VMEM_LIMIT),
        name="cond_linear",
    )(ct, w, b.reshape(1, n))


def _conv_rows(start, size):
    return pl.ds(2 * start, size, stride=2)


class _Mixer:
    def __init__(self, x_ref, mod, g1_ref, w_in_ref, b_in_ref, lng_ref, lnb_ref, wsp_ref, bsp_ref,
                 cw_ref, cb_ref, gng_ref, gnb_ref, ona_ref, onb_ref, w_out_ref, ypad_ref):
        self.x_ref = x_ref
        self.mod = mod
        self.r = dict(g1=g1_ref, w_in=w_in_ref, b_in=b_in_ref, lng=lng_ref, lnb=lnb_ref, wsp=wsp_ref, bsp=bsp_ref,
                      cw=cw_ref, cb=cb_ref, gng=gng_ref, gnb=gnb_ref, ona=ona_ref, onb=onb_ref, w_out=w_out_ref)
        self.ypad_ref = ypad_ref
        self.ts = x_ref.shape[1]
        self.conv_out = {}
        self.conv_prev = None
        self.yb = {}
        self.z = {}
        self.zb = {}

    def norm(self):
        shift1, scale1 = self.mod[0:1], self.mod[1:2]
        h = _rms(self.x_ref[0], self.r["g1"][...]) * (1.0 + scale1) + shift1
        self.h = h.astype(BF16)

    def in_proj(self, q):
        cs = slice(q * D_A, (q + 1) * D_A)
        self.z[q] = _dot(self.h, self.r["w_in"][:, cs]) + self.r["b_in"][:, cs]

    def in_glu(self, g):
        rs = slice(g * GLU_ROWS, (g + 1) * GLU_ROWS)
        cs = slice(2 * D_A, D_IN)
        self.zb[g] = _dot(self.h[rs], self.r["w_in"][:, cs]) + self.r["b_in"][:, cs]

    def gate_u(self):
        self.u = _gelu(self.z.pop(0))

    def gate_v(self):
        v = _gelu(self.z.pop(1))
        mu = jnp.mean(v, axis=-1, keepdims=True)
        vc = v - mu
        var = jnp.mean(vc * vc, axis=-1, keepdims=True)
        vn = vc * lax.rsqrt(var + EPS) * self.r["lng"][...] + self.r["lnb"][...]
        lane = lax.broadcasted_iota(jnp.int32, (1, D_A), 1)
        lo = (lane % LANES) < HEAD_DIM_A
        self.vn_lo = jnp.where(lo, vn, 0.0).astype(BF16)
        self.vn_hi = jnp.where(lo, 0.0, vn).astype(BF16)

    def glu_b(self, g):
        ts = self.ts
        zb = self.zb.pop(g)
        y = zb[:, 0:D_B] * jax.nn.sigmoid(zb[:, D_B:])
        if g == 0:
            for j in range(D_B // LANES):
                self.ypad_ref[j, _conv_rows(0, HALO), :] = self.ypad_ref[j, _conv_rows(ts, HALO), :]
        for j in range(D_B // LANES):
            self.ypad_ref[j, _conv_rows(HALO + g * GLU_ROWS, GLU_ROWS), :] = y[:, j * LANES:(j + 1) * LANES]

    def spatial(self):
        t_row = lax.broadcasted_iota(jnp.int32, (CHUNK, N_HEADS_A * CHUNK), 0)
        s_col = lax.broadcasted_iota(jnp.int32, (CHUNK, N_HEADS_A * CHUNK), 1) % CHUNK
        wsp = jnp.where(s_col <= t_row, self.r["wsp"][...], 0.0).astype(BF16)
        n_chunks = self.ts // CHUNK
        per_pair = []
        for j in range(D_A // LANES):
            cs = slice(j * LANES, (j + 1) * LANES)
            rhs = jnp.concatenate(
                [jnp.concatenate([self.vn_lo[c * CHUNK:(c + 1) * CHUNK, cs],
                                  self.vn_hi[c * CHUNK:(c + 1) * CHUNK, cs]], axis=0) for c in range(n_chunks)],
                axis=1)
            per_pair.append(_dot(wsp[:, 2 * j * CHUNK:(2 * j + 2) * CHUNK], rhs))
        self.mixed = [jnp.concatenate([p[:, c * LANES:(c + 1) * LANES] for p in per_pair], axis=1)
                      for c in range(n_chunks)]

    def mix_a(self):
        bsp = self.r["bsp"][...]
        mixed = jnp.concatenate([m + bsp for m in self.mixed], axis=0)
        self.ya = _rms(self.u * mixed, self.r["ona"][...]).astype(BF16)

    def conv(self, rb):
        cw = self.r["cw"][...]
        cb = self.r["cb"][...]
        off = HALO - (CONV_WIDTH - 1)
        cols = []
        for j in range(D_B // LANES):
            cs = slice(j * LANES, (j + 1) * LANES)
            parts = []
            for sub in range(CONV_RB // CONV_UNIT):
                r0 = rb * CONV_RB + sub * CONV_UNIT
                bias = jnp.broadcast_to(cb[:, cs], (SUBLANES, LANES))
                if self.conv_prev is not None:
                    p = self.conv_prev
                    never = (p[0:SUBLANES] > p[SUBLANES:]) & (p[0:SUBLANES] < p[SUBLANES:])
                    bias = jnp.where(never, p[0:SUBLANES], bias)
                acc = jnp.concatenate([bias] * (CONV_UNIT // SUBLANES), axis=0)
                for rho in range(min(SUBLANES, CONV_WIDTH)):
                    taps = range(rho, CONV_WIDTH, SUBLANES)
                    win = self.ypad_ref[j, _conv_rows(r0 + off + rho, CONV_UNIT + SUBLANES * (len(taps) - 1)), :]
                    for q, k in enumerate(taps):
                        acc = acc + cw[k:k + 1, cs] * win[SUBLANES * q:SUBLANES * q + CONV_UNIT]
                parts.append(acc)
                self.conv_prev = acc
            cols.append(jnp.concatenate(parts, axis=0))
        self.conv_out[rb] = jnp.concatenate(cols, axis=1)

    @staticmethod
    def _group_mean(a):
        lane = lax.broadcasted_iota(jnp.int32, (1, LANES), 1)
        lo = lane < GROUP_DIM_B
        cols = []
        for j in range(D_B // LANES):
            aj = a[:, j * LANES:(j + 1) * LANES]
            s_lo = jnp.sum(jnp.where(lo, aj, 0.0), axis=-1, keepdims=True)
            s_hi = jnp.sum(jnp.where(lo, 0.0, aj), axis=-1, keepdims=True)
            cols.append(jnp.where(lo, s_lo, s_hi))
        return jnp.concatenate(cols, axis=1) * (1.0 / GROUP_DIM_B)

    def norm_b(self, rb):
        yc = self.conv_out.pop(rb)
        ycc = yc - self._group_mean(yc)
        gvar = self._group_mean(ycc * ycc)
        yn = ycc * lax.rsqrt(gvar + EPS) * self.r["gng"][...] + self.r["gnb"][...]
        self.yb[rb] = _rms(yn * jax.nn.sigmoid(yn), self.r["onb"][...]).astype(BF16)

    def out_proj(self, x1_ref):
        w_out = self.r["w_out"]
        yb = jnp.concatenate([self.yb[rb] for rb in range(self.ts // CONV_RB)], axis=0)
        proj = _dot(self.ya, w_out[0:D_A, :]) + _dot(yb, w_out[D_A:, :])
        x1_ref[...] = self.x_ref[0] + self.mod[2:3] * proj


class _Ffn:
    def __init__(self, x_ref, mod, modo, g2_ref, gf_ref, w1_ref, w2_ref):
        self.x_ref, self.mod, self.modo = x_ref, mod, modo
        self.g2_ref, self.gf_ref, self.w1_ref, self.w2_ref = g2_ref, gf_ref, w1_ref, w2_ref
        self.gu = {}
        self.a = {}

    def norm(self):
        shift2, scale2 = self.mod[3:4], self.mod[4:5]
        self.x = self.x_ref[...]
        self.h = (_rms(self.x, self.g2_ref[...]) * (1.0 + scale2) + shift2).astype(BF16)

    def up(self, f):
        g = _dot(self.h, self.w1_ref[:, f * FFN_TF:(f + 1) * FFN_TF])
        up = _dot(self.h, self.w1_ref[:, D_FF + f * FFN_TF:D_FF + (f + 1) * FFN_TF])
        self.gu[f] = (g, up)

    def act(self, f):
        g, up = self.gu.pop(f)
        self.a[f] = (g * jax.nn.sigmoid(g) * up).astype(BF16)

    def down(self):
        a = jnp.concatenate([self.a.pop(f) for f in range(N_FF)], axis=1)
        self.acc = _dot(a, self.w2_ref[...])

    def finish(self, o_ref):
        x2 = self.x + self.mod[5:6] * self.acc
        o_ref[0] = _rms(x2, self.gf_ref[...]) * (1.0 + self.modo[1:2]) + self.modo[0:1]


def _block_kernel(n_seq_tiles, x_ref, modm_ref, modf_ref, modo_ref, g1_ref, w_in_ref, b_in_ref, lng_ref, lnb_ref,
                  wsp_ref, bsp_ref, cw_ref, cb_ref, gng_ref, gnb_ref, ona_ref, onb_ref, w_out_ref,
                  g2_ref, gf_ref, w1_ref, w2_ref, o_ref, x1_ref, ypad_ref):
    t = pl.program_id(0)
    n_tiles = pl.num_programs(0) - 1
    ts = x_ref.shape[1]
    s_idx = lax.rem(jnp.minimum(t, n_tiles - 1), n_seq_tiles)

    @pl.when(t == 0)
    def _():
        x1_ref[...] = jnp.zeros(x1_ref.shape, F32)

    @pl.when(s_idx == 0)
    def _():
        for j in range(D_B // LANES):
            ypad_ref[j, _conv_rows(ts, HALO), :] = jnp.zeros((HALO, LANES), F32)

    mx = _Mixer(x_ref, modm_ref[0], g1_ref, w_in_ref, b_in_ref, lng_ref, lnb_ref, wsp_ref, bsp_ref,
                cw_ref, cb_ref, gng_ref, gnb_ref, ona_ref, onb_ref, w_out_ref, ypad_ref)
    ff = _Ffn(x1_ref, modf_ref[0], modo_ref[0], g2_ref, gf_ref, w1_ref, w2_ref)

    def conv_group(g):
        for rb in range(g * GLU_ROWS // CONV_RB, (g + 1) * GLU_ROWS // CONV_RB):
            mx.conv(rb)
            mx.norm_b(rb)

    ff.norm(); mx.norm()
    ff.up(0)
    mx.in_glu(0); mx.glu_b(0)
    ff.up(1); ff.act(0)
    mx.in_proj(1)
    ff.up(2); ff.act(1)
    mx.gate_v()
    mx.in_proj(0)
    ff.up(3); ff.act(2)
    mx.gate_u()
    mx.spatial()
    ff.up(4); ff.act(3)
    mx.mix_a()
    mx.in_glu(1); mx.glu_b(1)
    ff.up(5); ff.act(4)
    ff.up(6); ff.act(5)
    ff.up(7); ff.act(6)
    ff.up(8); ff.act(7)
    ff.up(9); ff.act(8)
    ff.up(10); ff.act(9); ff.act(10)
    ff.down()
    conv_group(0)
    conv_group(1)
    mx.out_proj(x1_ref)
    ff.finish(o_ref)


def _block(x, mod, modo, g1, w_in, b_in, lng, lnb, wsp, bsp, cw, cb, gng, gnb, ona, onb, w_out, g2, gf, w1, w2):
    bsz, seq, d = x.shape
    ts = TILE
    n_s = seq // ts
    n_tiles = bsz * n_s
    assert ts // GLU_ROWS == 2 and N_FF == 11, "the stage order in _block_kernel is written for these counts"
    assert 2 * GROUP_DIM_B == LANES and 2 * HEAD_DIM_A == LANES, "two groups / heads per 128-lane column"
    mix_tile = lambda t: jnp.minimum(t, n_tiles - 1)
    ffn_tile = lambda t: jnp.maximum(t - 1, 0)
    const = lambda shape: pl.BlockSpec(shape, lambda t: (0,) * len(shape))
    return pl.pallas_call(
        functools.partial(_block_kernel, n_s),
        grid=(n_tiles + 1,),
        in_specs=[
            pl.BlockSpec((1, ts, d), lambda t: (mix_tile(t) // n_s, mix_tile(t) % n_s, 0)),
            pl.BlockSpec((1, N_MOD, d), lambda t: (mix_tile(t) // n_s, 0, 0)),
            pl.BlockSpec((1, N_MOD, d), lambda t: (ffn_tile(t) // n_s, 0, 0)),
            pl.BlockSpec((1, 2, d), lambda t: (ffn_tile(t) // n_s, 0, 0)),
            const((1, d)),
            const((d, D_IN)),
            const((1, D_IN)),
            const((1, D_A)),
            const((1, D_A)),
            const((CHUNK, N_HEADS_A * CHUNK)),
            const((CHUNK, D_A)),
            const((CONV_WIDTH, D_B)),
            const((1, D_B)),
            const((1, D_B)),
            const((1, D_B)),
            const((1, D_A)),
            const((1, D_B)),
            const((d, d)),
            const((1, d)),
            const((1, d)),
            const((d, 2 * D_FF)),
            const((D_FF, d)),
        ],
        out_specs=pl.BlockSpec((1, ts, d), lambda t: (ffn_tile(t) // n_s, ffn_tile(t) % n_s, 0)),
        out_shape=jax.ShapeDtypeStruct((bsz, seq, d), F32),
        scratch_shapes=[pltpu.VMEM((ts, d), F32),
                        pltpu.VMEM((D_B // LANES, 2 * (HALO + ts), LANES), F32)],
        compiler_params=pltpu.CompilerParams(
            dimension_semantics=("arbitrary",), vmem_limit_bytes=VMEM_LIMIT),
        name="block",
    )(x, mod, mod, modo, g1, w_in, b_in, lng, lnb, wsp, bsp, cw, cb, gng, gnb, ona, onb, w_out, g2, gf, w1, w2)


def kernel(x, c, ada_w, ada_b, norm1_g, w_in, b_in, a_ln_g, a_ln_b, a_spatial_w, a_spatial_b, b_conv_w, b_conv_b, b_gn_g, b_gn_b, out_norm_a_g, out_norm_b_g, w_out, norm2_g, w_ffn_in, w_ffn_out, ada_f_w, ada_f_b, norm_f_g):
    bsz = x.shape[0]
    assert ada_w.shape[0] == 1, "single-layer block"
    ct = c.T
    row = lambda a: a.reshape(1, -1)
    mod = _cond_linear(ct, ada_w[0], ada_b[0]).reshape(bsz, N_MOD, D_MODEL)
    modo = _cond_linear(ct, ada_f_w, ada_f_b).reshape(bsz, 2, D_MODEL)
    wsp = a_spatial_w[0].transpose(1, 0, 2).reshape(CHUNK, N_HEADS_A * CHUNK)
    bsp = jnp.repeat(a_spatial_b[0].T, HEAD_DIM_A, axis=1)
    return _block(x, mod, modo, row(norm1_g[0]), w_in[0].astype(BF16), row(b_in[0]), row(a_ln_g[0]),
                  row(a_ln_b[0]), wsp, bsp, b_conv_w[0], row(b_conv_b[0]), row(b_gn_g[0]), row(b_gn_b[0]),
                  row(out_norm_a_g[0]), row(out_norm_b_g[0]), w_out[0].astype(BF16),
                  row(norm2_g[0]), row(norm_f_g), w_ffn_in[0].astype(BF16), w_ffn_out[0].astype(BF16))
```

```python
import functools

import jax
import jax.numpy as jnp
from jax import lax
from jax.experimental import pallas as pl
from jax.experimental.pallas import tpu as pltpu

D_MODEL = 1024
D_A = 512
D_B = 512
N_HEADS_A = 8
HEAD_DIM_A = D_A // N_HEADS_A
N_GROUPS_B = 8
GROUP_DIM_B = D_B // N_GROUPS_B
CHUNK = 128
CONV_WIDTH = 31
D_IN = 2 * D_A + 2 * D_B
D_FF = 2816
N_MOD = 6
EPS = 1e-6

LANES = 128
SUBLANES = 8
HALO = 32
COND_TN = 1024
TILE = 512
FFN_TF = 256
N_FF = D_FF // FFN_TF
CONV_PITCH = 3
CONV_RB = 64
CONV_UNIT = 16
GLU_ROWS = 256
VMEM_LIMIT = 60 * 1024 * 1024

F32 = jnp.float32
BF16 = jnp.bfloat16


def _rms(x, g):
    ms = jnp.mean(x * x, axis=-1, keepdims=True)
    return x * lax.rsqrt(ms + EPS) * g


def _gelu(a):
    return 0.5 * a * (1.0 + lax.erf(a * 0.7071067811865476))


def _dot(a, b):
    return jnp.dot(a, b, preferred_element_type=F32)


def _cond_kernel(ct_ref, w_ref, b_ref, o_ref):
    ct = ct_ref[...]
    act = ct * jax.nn.sigmoid(ct)
    w = w_ref[...]
    rows = [jnp.sum(w * act[:, b:b + 1], axis=0, keepdims=True) for b in range(ct.shape[1])]
    o_ref[...] = jnp.concatenate(rows, axis=0) + b_ref[...]


def _cond_linear(ct, w, b):
    d, n = w.shape
    nb = ct.shape[1]
    return pl.pallas_call(
        _cond_kernel,
        grid=(n // COND_TN,),
        in_specs=[
            pl.BlockSpec((d, nb), lambda j: (0, 0)),
            pl.BlockSpec((d, COND_TN), lambda j: (0, j)),
            pl.BlockSpec((1, COND_TN), lambda j: (0, j)),
        ],
        out_specs=pl.BlockSpec((nb, COND_TN), lambda j: (0, j)),
        out_shape=jax.ShapeDtypeStruct((nb, n), F32),
        compiler_params=pltpu.CompilerParams(dimension_semantics=("arbitrary",), vmem_limit_bytes=VMEM_LIMIT),
        name="cond_linear",
    )(ct, w, b.reshape(1, n))


def _conv_rows(start, size):
    return pl.ds(CONV_PITCH * start, size, stride=CONV_PITCH)


class _Mixer:
    def __init__(self, x_ref, mod, g1_ref, w_in_ref, b_in_ref, lng_ref, lnb_ref, wsp_ref, bsp_ref,
                 cw_ref, cb_ref, gng_ref, gnb_ref, ona_ref, onb_ref, w_out_ref, ypad_ref, h_ref, yab_ref):
        self.x_ref = x_ref
        self.mod = mod
        self.r = dict(g1=g1_ref, w_in=w_in_ref, b_in=b_in_ref, lng=lng_ref, lnb=lnb_ref, wsp=wsp_ref, bsp=bsp_ref,
                      cw=cw_ref, cb=cb_ref, gng=gng_ref, gnb=gnb_ref, ona=ona_ref, onb=onb_ref, w_out=w_out_ref)
        self.ypad_ref = ypad_ref
        self.h_ref = h_ref
        self.yab_ref = yab_ref
        self.ts = x_ref.shape[1]
        self.conv_out = {}
        self.conv_prev = None
        self.z = {}
        self.zb = {}

    def norm(self):
        shift1, scale1 = self.mod[0:1], self.mod[1:2]
        h = _rms(self.x_ref[0], self.r["g1"][...]) * (1.0 + scale1) + shift1
        self.h_ref[...] = h.astype(BF16)

    def in_proj(self, q):
        cs = slice(q * D_A, (q + 1) * D_A)
        self.z[q] = _dot(self.h_ref[...], self.r["w_in"][:, cs]) + self.r["b_in"][:, cs]

    def in_glu(self, g):
        rs = slice(g * GLU_ROWS, (g + 1) * GLU_ROWS)
        cs = slice(2 * D_A, D_IN)
        self.zb[g] = _dot(self.h_ref[rs, :], self.r["w_in"][:, cs]) + self.r["b_in"][:, cs]

    def gate_u(self):
        self.u = _gelu(self.z.pop(0))

    def gate_v(self):
        v = _gelu(self.z.pop(1))
        mu = jnp.mean(v, axis=-1, keepdims=True)
        vc = v - mu
        var = jnp.mean(vc * vc, axis=-1, keepdims=True)
        vn = vc * lax.rsqrt(var + EPS) * self.r["lng"][...] + self.r["lnb"][...]
        lane = lax.broadcasted_iota(jnp.int32, (1, D_A), 1)
        lo = (lane % LANES) < HEAD_DIM_A
        self.vn_lo = jnp.where(lo, vn, 0.0).astype(BF16)
        self.vn_hi = jnp.where(lo, 0.0, vn).astype(BF16)

    def glu_b(self, g):
        ts = self.ts
        zb = self.zb.pop(g)
        y = zb[:, 0:D_B] * jax.nn.sigmoid(zb[:, D_B:])
        if g == 0:
            for j in range(D_B // LANES):
                self.ypad_ref[j, _conv_rows(0, HALO), :] = self.ypad_ref[j, _conv_rows(ts, HALO), :]
        for j in range(D_B // LANES):
            self.ypad_ref[j, _conv_rows(HALO + g * GLU_ROWS, GLU_ROWS), :] = y[:, j * LANES:(j + 1) * LANES]

    def spatial(self):
        t_row = lax.broadcasted_iota(jnp.int32, (CHUNK, N_HEADS_A * CHUNK), 0)
        s_col = lax.broadcasted_iota(jnp.int32, (CHUNK, N_HEADS_A * CHUNK), 1) % CHUNK
        wsp = jnp.where(s_col <= t_row, self.r["wsp"][...], 0.0).astype(BF16)
        n_chunks = self.ts // CHUNK
        per_pair = []
        for j in range(D_A // LANES):
            cs = slice(j * LANES, (j + 1) * LANES)
            rhs = jnp.concatenate(
                [jnp.concatenate([self.vn_lo[c * CHUNK:(c + 1) * CHUNK, cs],
                                  self.vn_hi[c * CHUNK:(c + 1) * CHUNK, cs]], axis=0) for c in range(n_chunks)],
                axis=1)
            per_pair.append(_dot(wsp[:, 2 * j * CHUNK:(2 * j + 2) * CHUNK], rhs))
        self.mixed = [jnp.concatenate([p[:, c * LANES:(c + 1) * LANES] for p in per_pair], axis=1)
                      for c in range(n_chunks)]

    def mix_a(self):
        bsp = self.r["bsp"][...]
        mixed = jnp.concatenate([m + bsp for m in self.mixed], axis=0)
        self.yab_ref[:, 0:D_A] = _rms(self.u * mixed, self.r["ona"][...]).astype(BF16)

    def conv(self, rb):
        cw = self.r["cw"][...]
        cb = self.r["cb"][...]
        off = HALO - (CONV_WIDTH - 1)
        cols = []
        for j in range(D_B // LANES):
            cs = slice(j * LANES, (j + 1) * LANES)
            parts = []
            for sub in range(CONV_RB // CONV_UNIT):
                r0 = rb * CONV_RB + sub * CONV_UNIT
                bias = jnp.broadcast_to(cb[:, cs], (SUBLANES, LANES))
                if self.conv_prev is not None:
                    p = self.conv_prev
                    never = (p[0:SUBLANES] > p[SUBLANES:]) & (p[0:SUBLANES] < p[SUBLANES:])
                    bias = jnp.where(never, p[0:SUBLANES], bias)
                acc = jnp.concatenate([bias] * (CONV_UNIT // SUBLANES), axis=0)
                for rho in range(min(SUBLANES, CONV_WIDTH)):
                    taps = range(rho, CONV_WIDTH, SUBLANES)
                    win = self.ypad_ref[j, _conv_rows(r0 + off + rho, CONV_UNIT + SUBLANES * (len(taps) - 1)), :]
                    for q, k in enumerate(taps):
                        acc = acc + cw[k:k + 1, cs] * win[SUBLANES * q:SUBLANES * q + CONV_UNIT]
                parts.append(acc)
                self.conv_prev = acc
            cols.append(jnp.concatenate(parts, axis=0))
        self.conv_out[rb] = jnp.concatenate(cols, axis=1)

    @staticmethod
    def _group_mean(a):
        lane = lax.broadcasted_iota(jnp.int32, (1, LANES), 1)
        lo = lane < GROUP_DIM_B
        cols = []
        for j in range(D_B // LANES):
            aj = a[:, j * LANES:(j + 1) * LANES]
            s_lo = jnp.sum(jnp.where(lo, aj, 0.0), axis=-1, keepdims=True)
            s_hi = jnp.sum(jnp.where(lo, 0.0, aj), axis=-1, keepdims=True)
            cols.append(jnp.where(lo, s_lo, s_hi))
        return jnp.concatenate(cols, axis=1) * (1.0 / GROUP_DIM_B)

    def norm_b(self, rb):
        yc = self.conv_out.pop(rb)
        ycc = yc - self._group_mean(yc)
        gvar = self._group_mean(ycc * ycc)
        yn = ycc * lax.rsqrt(gvar + EPS) * self.r["gng"][...] + self.r["gnb"][...]
        yb = _rms(yn * jax.nn.sigmoid(yn), self.r["onb"][...]).astype(BF16)
        self.yab_ref[rb * CONV_RB:(rb + 1) * CONV_RB, D_A:] = yb

    def out_proj(self, x1_ref):
        proj = _dot(self.yab_ref[...], self.r["w_out"][...])
        x1_ref[...] = self.x_ref[0] + self.mod[2:3] * proj


class _Ffn:
    def __init__(self, x_ref, mod, modo, g2_ref, gf_ref, w1_ref, w2_ref, h_ref, a_ref):
        self.x_ref, self.mod, self.modo = x_ref, mod, modo
        self.g2_ref, self.gf_ref, self.w1_ref, self.w2_ref = g2_ref, gf_ref, w1_ref, w2_ref
        self.h_ref, self.a_ref = h_ref, a_ref
        self.gu = {}

    def norm(self):
        shift2, scale2 = self.mod[3:4], self.mod[4:5]
        self.x = self.x_ref[...]
        self.h_ref[...] = (_rms(self.x, self.g2_ref[...]) * (1.0 + scale2) + shift2).astype(BF16)

    def up(self, f):
        g = _dot(self.h_ref[...], self.w1_ref[:, f * FFN_TF:(f + 1) * FFN_TF])
        up = _dot(self.h_ref[...], self.w1_ref[:, D_FF + f * FFN_TF:D_FF + (f + 1) * FFN_TF])
        self.gu[f] = (g, up)

    def act(self, f):
        g, up = self.gu.pop(f)
        self.a_ref[:, f * FFN_TF:(f + 1) * FFN_TF] = (g * jax.nn.sigmoid(g) * up).astype(BF16)

    def down(self):
        self.acc = _dot(self.a_ref[...], self.w2_ref[...])

    def finish(self, o_ref):
        x2 = self.x + self.mod[5:6] * self.acc
        o_ref[0] = _rms(x2, self.gf_ref[...]) * (1.0 + self.modo[1:2]) + self.modo[0:1]


def _block_kernel(n_seq_tiles, x_ref, modm_ref, modf_ref, modo_ref, g1_ref, w_in_ref, b_in_ref, lng_ref, lnb_ref,
                  wsp_ref, bsp_ref, cw_ref, cb_ref, gng_ref, gnb_ref, ona_ref, onb_ref, w_out_ref,
                  g2_ref, gf_ref, w1_ref, w2_ref, o_ref, x1_ref, ypad_ref, hm_ref, yab_ref, hf_ref, a_ref):
    t = pl.program_id(0)
    n_tiles = pl.num_programs(0) - 1
    ts = x_ref.shape[1]
    s_idx = lax.rem(jnp.minimum(t, n_tiles - 1), n_seq_tiles)

    @pl.when(t == 0)
    def _():
        x1_ref[...] = jnp.zeros(x1_ref.shape, F32)

    @pl.when(s_idx == 0)
    def _():
        for j in range(D_B // LANES):
            ypad_ref[j, _conv_rows(ts, HALO), :] = jnp.zeros((HALO, LANES), F32)

    mx = _Mixer(x_ref, modm_ref[0], g1_ref, w_in_ref, b_in_ref, lng_ref, lnb_ref, wsp_ref, bsp_ref,
                cw_ref, cb_ref, gng_ref, gnb_ref, ona_ref, onb_ref, w_out_ref, ypad_ref, hm_ref, yab_ref)
    ff = _Ffn(x1_ref, modf_ref[0], modo_ref[0], g2_ref, gf_ref, w1_ref, w2_ref, hf_ref, a_ref)

    def conv_group(g):
        for rb in range(g * GLU_ROWS // CONV_RB, (g + 1) * GLU_ROWS // CONV_RB):
            mx.conv(rb)
            mx.norm_b(rb)

    ff.norm(); mx.norm()
    ff.up(0)
    mx.in_glu(0); mx.glu_b(0)
    ff.up(1); ff.act(0)
    mx.in_proj(1)
    ff.up(2); ff.act(1)
    mx.gate_v()
    mx.in_proj(0)
    ff.up(3); ff.act(2)
    mx.gate_u()
    mx.spatial()
    ff.up(4); ff.act(3)
    mx.mix_a()
    mx.in_glu(1); mx.glu_b(1)
    ff.up(5); ff.act(4)
    ff.up(6); ff.act(5)
    ff.up(7); ff.act(6)
    ff.up(8); ff.act(7)
    ff.up(9); ff.act(8)
    ff.up(10); ff.act(9); ff.act(10)
    ff.down()
    conv_group(0)
    conv_group(1)
    mx.out_proj(x1_ref)
    ff.finish(o_ref)


def _block(x, mod, modo, g1, w_in, b_in, lng, lnb, wsp, bsp, cw, cb, gng, gnb, ona, onb, w_out, g2, gf, w1, w2):
    bsz, seq, d = x.shape
    ts = TILE
    n_s = seq // ts
    n_tiles = bsz * n_s
    assert ts // GLU_ROWS == 2 and N_FF == 11, "the stage order in _block_kernel is written for these counts"
    assert 2 * GROUP_DIM_B == LANES and 2 * HEAD_DIM_A == LANES, "two groups / heads per 128-lane column"
    mix_tile = lambda t: jnp.minimum(t, n_tiles - 1)
    ffn_tile = lambda t: jnp.maximum(t - 1, 0)
    const = lambda shape: pl.BlockSpec(shape, lambda t: (0,) * len(shape))
    return pl.pallas_call(
        functools.partial(_block_kernel, n_s),
        grid=(n_tiles + 1,),
        in_specs=[
            pl.BlockSpec((1, ts, d), lambda t: (mix_tile(t) // n_s, mix_tile(t) % n_s, 0)),
            pl.BlockSpec((1, N_MOD, d), lambda t: (mix_tile(t) // n_s, 0, 0)),
            pl.BlockSpec((1, N_MOD, d), lambda t: (ffn_tile(t) // n_s, 0, 0)),
            pl.BlockSpec((1, 2, d), lambda t: (ffn_tile(t) // n_s, 0, 0)),
            const((1, d)),
            const((d, D_IN)),
            const((1, D_IN)),
            const((1, D_A)),
            const((1, D_A)),
            const((CHUNK, N_HEADS_A * CHUNK)),
            const((CHUNK, D_A)),
            const((CONV_WIDTH, D_B)),
            const((1, D_B)),
            const((1, D_B)),
            const((1, D_B)),
            const((1, D_A)),
            const((1, D_B)),
            const((d, d)),
            const((1, d)),
            const((1, d)),
            const((d, 2 * D_FF)),
            const((D_FF, d)),
        ],
        out_specs=pl.BlockSpec((1, ts, d), lambda t: (ffn_tile(t) // n_s, ffn_tile(t) % n_s, 0)),
        out_shape=jax.ShapeDtypeStruct((bsz, seq, d), F32),
        scratch_shapes=[pltpu.VMEM((ts, d), F32),
                        pltpu.VMEM((D_B // LANES, CONV_PITCH * (HALO + ts), LANES), F32),
                        pltpu.VMEM((ts, d), BF16),
                        pltpu.VMEM((ts, D_A + D_B), BF16),
                        pltpu.VMEM((ts, d), BF16),
                        pltpu.VMEM((ts, D_FF), BF16)],
        compiler_params=pltpu.CompilerParams(
            dimension_semantics=("arbitrary",), vmem_limit_bytes=VMEM_LIMIT),
        name="block",
    )(x, mod, mod, modo, g1, w_in, b_in, lng, lnb, wsp, bsp, cw, cb, gng, gnb, ona, onb, w_out, g2, gf, w1, w2)


def kernel(x, c, ada_w, ada_b, norm1_g, w_in, b_in, a_ln_g, a_ln_b, a_spatial_w, a_spatial_b, b_conv_w, b_conv_b, b_gn_g, b_gn_b, out_norm_a_g, out_norm_b_g, w_out, norm2_g, w_ffn_in, w_ffn_out, ada_f_w, ada_f_b, norm_f_g):
    bsz = x.shape[0]
    assert ada_w.shape[0] == 1, "single-layer block"
    ct = c.T
    row = lambda a: a.reshape(1, -1)
    mod = _cond_linear(ct, ada_w[0], ada_b[0]).reshape(bsz, N_MOD, D_MODEL)
    modo = _cond_linear(ct, ada_f_w, ada_f_b).reshape(bsz, 2, D_MODEL)
    wsp = a_spatial_w[0].transpose(1, 0, 2).reshape(CHUNK, N_HEADS_A * CHUNK)
    bsp = jnp.repeat(a_spatial_b[0].T, HEAD_DIM_A, axis=1)
    return _block(x, mod, modo, row(norm1_g[0]), w_in[0].astype(BF16), row(b_in[0]), row(a_ln_g[0]),
                  row(a_ln_b[0]), wsp, bsp, b_conv_w[0], row(b_conv_b[0]), row(b_gn_g[0]), row(b_gn_b[0]),
                  row(out_norm_a_g[0]), row(out_norm_b_g[0]), w_out[0].astype(BF16),
                  row(norm2_g[0]), row(norm_f_g), w_ffn_in[0].astype(BF16), w_ffn_out[0].astype(BF16))
```
